```python
import functools
import jax, jax.numpy as jnp
from jax import lax
import numpy as np


D_MODEL = 1024
BATCH = 2
SEQ = 8192
DEPTH = 1
DEC_BATCH = 128
DEC_SEQ = 8
PAST_LEN = 2048
PAGE_SIZE = 128

N_META = 16
MIX_WIDTH = D_MODEL
ATTN_WIDTH = MIX_WIDTH // 2
LRU_WIDTH = MIX_WIDTH - ATTN_WIDTH
HEAD_DIM = 64
N_HEADS = ATTN_WIDTH // HEAD_DIM
LRU_BLOCKS = 8
LRU_BLOCK_W = LRU_WIDTH // LRU_BLOCKS
LRU_C = 8.0
CONV_WIDTH = 4
D_FF = 4 * D_MODEL
Q_BLOCK = 128
RMS_EPS = 1e-6
SB_BIAS_INIT = -6.0

kernel_name = "hymba_stickbreak_rglru_decode_step"


def rmsnorm(x, g):
    xf = x.astype(jnp.float32)
    y = xf * lax.rsqrt(jnp.mean(xf * xf, axis=-1, keepdims=True) + RMS_EPS) * g.astype(jnp.float32)
    return y.astype(x.dtype)


def sb_attend(q, q_pos, k, v, k_pos, bias):
    z = jnp.einsum('bqhd,bkhd->bhqk', q.astype(jnp.float32), k.astype(jnp.float32)) * (HEAD_DIM ** -0.5)
    z = z + bias.astype(jnp.float32)[None, :, None, None]
    mask = k_pos[None, :] < q_pos[:, None]
    log_keep = jnp.where(mask, jax.nn.log_sigmoid(-z), 0.0)
    suffix = lax.cumsum(log_keep, axis=3, reverse=True) - log_keep
    w = jnp.where(mask, jnp.exp(jax.nn.log_sigmoid(z) + suffix), 0.0)
    return jnp.einsum('bhqk,bkhd->bqhd', w, v.astype(jnp.float32)).astype(v.dtype)


def sb_prompt(q, k, v, bias):
    B, T = q.shape[0], q.shape[1]
    pos = jnp.arange(T)
    out_meta = sb_attend(q[:, :N_META], pos[:N_META], k, v, pos, bias)
    n_blk = (T - N_META) // Q_BLOCK
    qr = q[:, N_META:].reshape(B, n_blk, Q_BLOCK, N_HEADS, HEAD_DIM).swapaxes(0, 1)
    pr = pos[N_META:].reshape(n_blk, Q_BLOCK)
    out = lax.map(lambda a: sb_attend(a[0], a[1], k, v, pos, bias), (qr, pr))
    out = out.swapaxes(0, 1).reshape(B, T - N_META, N_HEADS, HEAD_DIM)
    return jnp.concatenate([out_meta, out], axis=1)


def sb_sample(q, k, v, bias, cache_k, cache_v, page_table):
    Bd, Tn = q.shape[0], q.shape[1]
    past = page_table.shape[1] * cache_k.shape[1]
    k_past = cache_k[page_table].reshape(Bd, past, N_HEADS, HEAD_DIM).astype(k.dtype)
    v_past = cache_v[page_table].reshape(Bd, past, N_HEADS, HEAD_DIM).astype(v.dtype)
    k_all = jnp.concatenate([k_past, k], axis=1)
    v_all = jnp.concatenate([v_past, v], axis=1)
    pos = jnp.arange(past + Tn)
    return sb_attend(q, pos[past:], k_all, v_all, pos, bias)


def rglru_branch(xl, conv_prev, h_prev, conv_w, conv_b, w_gate_a, b_gate_a, w_gate_x, b_gate_x, lru_lambda):
    B, T, W = xl.shape
    xpad = jnp.concatenate([conv_prev.astype(xl.dtype), xl], axis=1)
    xc = conv_b
    for j in range(CONV_WIDTH):
        xc = xc + xpad[:, j:j + T] * conv_w[j]
    conv_last = xpad[:, -(CONV_WIDTH - 1):]
    xb = xc.reshape(B, T, LRU_BLOCKS, LRU_BLOCK_W)
    r = jax.nn.sigmoid(jnp.einsum('btnc,ncd->btnd', xb, w_gate_a).reshape(B, T, W) + b_gate_a)
    i = jax.nn.sigmoid(jnp.einsum('btnc,ncd->btnd', xb, w_gate_x).reshape(B, T, W) + b_gate_x)
    log_a = -LRU_C * jax.nn.softplus(-lru_lambda.astype(jnp.float32)) * r.astype(jnp.float32)
    a = jnp.exp(log_a)
    b = jnp.sqrt(-jnp.expm1(2.0 * log_a)) * (i * xc).astype(jnp.float32)

    def step(h, ab):
        h = ab[0] * h + ab[1]
        return h, h

    h_last, hs = lax.scan(step, h_prev.astype(jnp.float32), (a.swapaxes(0, 1), b.swapaxes(0, 1)))
    return hs.swapaxes(0, 1).astype(xl.dtype), h_last, conv_last


def layer(x, attend, conv_prev, h_prev, g_mix_pre, g_mix_post, g_mlp_pre, g_mlp_post, w_in,
          conv_w, conv_b, w_gate_a, b_gate_a, w_gate_x, b_gate_x, lru_lambda, w_out, w_up, w_down):
    B, T, _ = x.shape
    hn = rmsnorm(x, g_mix_pre)
    proj = hn @ w_in
    q, k, v, xl, gate = jnp.split(
        proj, [ATTN_WIDTH, 2 * ATTN_WIDTH, 3 * ATTN_WIDTH, 3 * ATTN_WIDTH + LRU_WIDTH], axis=-1)
    q = q.reshape(B, T, N_HEADS, HEAD_DIM)
    k = k.reshape(B, T, N_HEADS, HEAD_DIM)
    v = v.reshape(B, T, N_HEADS, HEAD_DIM)
    attn = attend(q, k, v)
    lru, h_last, conv_last = rglru_branch(xl, conv_prev, h_prev, conv_w, conv_b,
                                          w_gate_a, b_gate_a, w_gate_x, b_gate_x, lru_lambda)
    mixed = jnp.concatenate([attn.reshape(B, T, ATTN_WIDTH), lru * jax.nn.gelu(gate)], axis=-1) @ w_out
    x = x + rmsnorm(mixed, g_mix_post)
    u = jax.nn.relu(rmsnorm(x, g_mlp_pre) @ w_up)
    x = x + rmsnorm((u * u) @ w_down, g_mlp_post)
    return x, k, v, h_last, conv_last


def setup_inputs(seed: int = 0) -> dict:
    key = jax.random.key(seed)
    ks = jax.random.split(key, 24)
    n_pages = PAST_LEN // PAGE_SIZE
    n_pool = (DEC_BATCH * n_pages * 5) // 4
    f32 = jnp.float32
    nrm = lambda k, s, sc: jax.random.normal(k, s, f32) * sc
    page_table = jax.random.permutation(ks[0], n_pool)[:DEC_BATCH * n_pages].reshape(DEC_BATCH, n_pages).astype(jnp.int32)
    u = jax.random.uniform(ks[1], (DEPTH, LRU_WIDTH), f32, 0.9, 0.999)
    s = u ** (1.0 / LRU_C)
    lru_lambda = jnp.log(s) - jnp.log1p(-s)
    return {
        'x_prompt': nrm(ks[2], (BATCH, SEQ, D_MODEL), 1.0),
        'x_sample': nrm(ks[3], (DEC_BATCH, DEC_SEQ, D_MODEL), 1.0),
        'cache_k': nrm(ks[4], (DEPTH, n_pool, PAGE_SIZE, N_HEADS, HEAD_DIM), 1.0),
        'cache_v': nrm(ks[5], (DEPTH, n_pool, PAGE_SIZE, N_HEADS, HEAD_DIM), 1.0),
        'state_h': nrm(ks[6], (DEPTH, DEC_BATCH, LRU_WIDTH), 0.5),
        'state_conv': nrm(ks[7], (DEPTH, DEC_BATCH, CONV_WIDTH - 1, LRU_WIDTH), 1.0),
        'page_table': page_table,
        'meta_tokens': nrm(ks[8], (N_META, D_MODEL), 1.0),
        'g_mix_pre': 1.0 + nrm(ks[9], (DEPTH, D_MODEL), 0.05),
        'g_mix_post': 1.0 + nrm(ks[10], (DEPTH, D_MODEL), 0.05),
        'g_mlp_pre': 1.0 + nrm(ks[11], (DEPTH, D_MODEL), 0.05),
        'g_mlp_post': 1.0 + nrm(ks[12], (DEPTH, D_MODEL), 0.05),
        'w_in': nrm(ks[13], (DEPTH, D_MODEL, 3 * ATTN_WIDTH + 2 * LRU_WIDTH), D_MODEL ** -0.5),
        'sb_bias': SB_BIAS_INIT + nrm(ks[23], (DEPTH, N_HEADS), 0.1),
        'conv_w': nrm(ks[14], (DEPTH, CONV_WIDTH, LRU_WIDTH), CONV_WIDTH ** -0.5),
        'conv_b': nrm(ks[15], (DEPTH, LRU_WIDTH), 0.01),
        'w_gate_a': nrm(ks[16], (DEPTH, LRU_BLOCKS, LRU_BLOCK_W, LRU_BLOCK_W), LRU_BLOCK_W ** -0.5),
        'b_gate_a': nrm(ks[17], (DEPTH, LRU_WIDTH), 0.1),
        'w_gate_x': nrm(ks[18], (DEPTH, LRU_BLOCKS, LRU_BLOCK_W, LRU_BLOCK_W), LRU_BLOCK_W ** -0.5),
        'b_gate_x': nrm(ks[19], (DEPTH, LRU_WIDTH), 0.1),
        'lru_lambda': lru_lambda,
        'w_out': nrm(ks[20], (DEPTH, MIX_WIDTH, D_MODEL), MIX_WIDTH ** -0.5),
        'w_up': nrm(ks[21], (DEPTH, D_MODEL, D_FF), D_MODEL ** -0.5),
        'w_down': nrm(ks[22], (DEPTH, D_FF, D_MODEL), D_FF ** -0.5),
    }


def reference(x_prompt, x_sample, cache_k, cache_v, state_h, state_conv, page_table, meta_tokens,
              g_mix_pre, g_mix_post, g_mlp_pre, g_mlp_post, w_in, sb_bias, conv_w, conv_b, w_gate_a, b_gate_a,
              w_gate_x, b_gate_x, lru_lambda, w_out, w_up, w_down):
    B = x_prompt.shape[0]
    meta = jnp.broadcast_to(meta_tokens.astype(x_prompt.dtype)[None], (B, N_META, D_MODEL))
    xp = jnp.concatenate([meta, x_prompt], axis=1)
    xs = x_sample
    kp_l, vp_l, hp_l, cp_l, ks_l, vs_l, hs_l, cs_l = [], [], [], [], [], [], [], []
    for l in range(DEPTH):
        lw = (g_mix_pre[l], g_mix_post[l], g_mlp_pre[l], g_mlp_post[l], w_in[l], conv_w[l], conv_b[l],
              w_gate_a[l], b_gate_a[l], w_gate_x[l], b_gate_x[l], lru_lambda[l], w_out[l], w_up[l], w_down[l])
        conv0 = jnp.zeros((B, CONV_WIDTH - 1, LRU_WIDTH), xp.dtype)
        h0 = jnp.zeros((B, LRU_WIDTH), jnp.float32)
        attend_p = functools.partial(sb_prompt, bias=sb_bias[l])
        xp, kp, vp, hp, cp = layer(xp, attend_p, conv0, h0, *lw)
        attend_s = functools.partial(sb_sample, bias=sb_bias[l], cache_k=cache_k[l], cache_v=cache_v[l],
                                     page_table=page_table)
        xs, kn, vn, hn, cn = layer(xs, attend_s, state_conv[l], state_h[l], *lw)
        kp_l.append(kp); vp_l.append(vp); hp_l.append(hp); cp_l.append(cp)
        ks_l.append(kn); vs_l.append(vn); hs_l.append(hn); cs_l.append(cn)
    y_prompt = xp[:, N_META:]
    y_sample = xs
    k_prompt = jnp.stack(kp_l, 0)
    v_prompt = jnp.stack(vp_l, 0)
    h_prompt = jnp.stack(hp_l, 0)
    conv_prompt = jnp.stack(cp_l, 0)
    k_sample = jnp.stack(ks_l, 0)
    v_sample = jnp.stack(vs_l, 0)
    h_sample = jnp.stack(hs_l, 0)
    conv_sample = jnp.stack(cs_l, 0)
    return (y_prompt, y_sample, k_prompt, v_prompt, h_prompt, conv_prompt, k_sample, v_sample, h_sample, conv_sample)
```

```python
import functools

import jax
import jax.numpy as jnp
from jax import lax
from jax.experimental import pallas as pl
from jax.experimental.pallas import tpu as pltpu

F32 = jnp.float32
BF16 = jnp.bfloat16

D_MODEL = 1024
N_META = 16
ATTN_WIDTH = 512
LRU_WIDTH = 512
HEAD_DIM = 64
N_HEADS = 8
LRU_BLOCKS = 8
LRU_C = 8.0
CONV_WIDTH = 4
D_FF = 4096
RMS_EPS = 1e-6
PROJ_WIDTH = 3 * ATTN_WIDTH + 2 * LRU_WIDTH

SUBLANES = 8
ROWS = 256
HEAD_PAIR = 2 * HEAD_DIM
FF_CHUNK = 1024
MIB = 1024 * 1024


def _rmsnorm(x, g):
    ms = jnp.mean(x * x, axis=-1, keepdims=True)
    return x * lax.rsqrt(ms + RMS_EPS) * g


def _gelu_tanh(x):
    return x * (0.5 * (1.0 + jnp.tanh(0.7978845608028654 * (x + 0.044715 * (x * x * x)))))


def _sigmoid(x):
    return 1.0 / (1.0 + jnp.exp(-x))


def _softplus(x):
    return jnp.maximum(x, 0.0) + jnp.log(1.0 + jnp.exp(-jnp.abs(x)))


def _inproj(x, g, w):
    hn = _rmsnorm(x, g).astype(BF16)
    return jnp.dot(hn, w, preferred_element_type=F32)


def _inproj_prompt_kernel(x_ref, meta_ref, g_ref, w_ref,
                          q_ref, kt_ref, vb_ref, k_ref, v_ref, xl_ref, gate_ref):
    x = jnp.where(pl.program_id(1) == 0, meta_ref[...], x_ref[0])
    proj = _inproj(x, g_ref[...], w_ref[...])
    k = proj[:, ATTN_WIDTH:2 * ATTN_WIDTH]
    v = proj[:, 2 * ATTN_WIDTH:3 * ATTN_WIDTH]
    q_ref[0] = (proj[:, :ATTN_WIDTH] * (HEAD_DIM ** -0.5)).astype(BF16)
    kt_ref[0] = k.T.astype(BF16)
    vb_ref[0] = v.astype(BF16)
    k_ref[0] = k
    v_ref[0] = v
    xl_ref[0] = proj[:, 3 * ATTN_WIDTH:3 * ATTN_WIDTH + LRU_WIDTH]
    gate_ref[0] = proj[:, 3 * ATTN_WIDTH + LRU_WIDTH:]


def _inproj_sample_kernel(x_ref, g_ref, w_ref, q_ref, k_ref, v_ref, xl_ref, gate_ref):
    proj = _inproj(x_ref[...], g_ref[...], w_ref[...])
    q_ref[...] = (proj[:, :ATTN_WIDTH] * (HEAD_DIM ** -0.5)).astype(BF16)
    k_ref[...] = proj[:, ATTN_WIDTH:2 * ATTN_WIDTH]
    v_ref[...] = proj[:, 2 * ATTN_WIDTH:3 * ATTN_WIDTH]
    xl_ref[...] = proj[:, 3 * ATTN_WIDTH:3 * ATTN_WIDTH + LRU_WIDTH]
    gate_ref[...] = proj[:, 3 * ATTN_WIDTH + LRU_WIDTH:]


def _const_spec(shape):
    return pl.BlockSpec(shape, lambda *_: (0,) * len(shape))


def _inproj_prompt(x_prompt, meta_blk, g, w_in, n_blk):
    batch, _, _ = x_prompt.shape
    t_pad = n_blk * ROWS

    def rows(width, dtype):
        return jax.ShapeDtypeStruct((batch, t_pad, width), dtype)

    row_spec = pl.BlockSpec((1, ROWS, ATTN_WIDTH), lambda b, i: (b, i, 0))
    return pl.pallas_call(
        _inproj_prompt_kernel,
        grid=(batch, n_blk),
        in_specs=[
            pl.BlockSpec((1, ROWS, D_MODEL), lambda b, i: (b, jnp.maximum(i - 1, 0), 0)),
            _const_spec((ROWS, D_MODEL)),
            _const_spec((1, D_MODEL)),
            _const_spec((D_MODEL, PROJ_WIDTH)),
        ],
        out_specs=[
            row_spec,
            pl.BlockSpec((1, ATTN_WIDTH, ROWS), lambda b, i: (b, 0, i)),
            row_spec, row_spec, row_spec, row_spec, row_spec,
        ],
        out_shape=[
            rows(ATTN_WIDTH, BF16),
            jax.ShapeDtypeStruct((batch, ATTN_WIDTH, t_pad), BF16),
            rows(ATTN_WIDTH, BF16), rows(ATTN_WIDTH, F32), rows(ATTN_WIDTH, F32),
            rows(LRU_WIDTH, F32), rows(LRU_WIDTH, F32),
        ],
        compiler_params=pltpu.CompilerParams(
            dimension_semantics=("arbitrary", "arbitrary"), vmem_limit_bytes=48 * MIB),
        name="inproj_prompt",
    )(x_prompt, meta_blk, g, w_in)


def _inproj_sample(x, g, w_in):
    n = x.shape[0]
    row_spec = pl.BlockSpec((ROWS, ATTN_WIDTH), lambda i: (i, 0))
    f32_rows = jax.ShapeDtypeStruct((n, ATTN_WIDTH), F32)
    return pl.pallas_call(
        _inproj_sample_kernel,
        grid=(n // ROWS,),
        in_specs=[
            pl.BlockSpec((ROWS, D_MODEL), lambda i: (i, 0)),
            _const_spec((1, D_MODEL)),
            _const_spec((D_MODEL, PROJ_WIDTH)),
        ],
        out_specs=[row_spec] * 5,
        out_shape=[jax.ShapeDtypeStruct((n, ATTN_WIDTH), BF16), f32_rows, f32_rows, f32_rows, f32_rows],
        compiler_params=pltpu.CompilerParams(
            dimension_semantics=("arbitrary",), vmem_limit_bytes=48 * MIB),
        name="inproj_sample",
    )(x, g, w_in)


def _lru_coeffs(xc, wg, bg, lam):
    gates = jnp.dot(xc.astype(BF16), wg, preferred_element_type=F32) + bg
    r = _sigmoid(gates[:, :LRU_WIDTH])
    i = _sigmoid(gates[:, LRU_WIDTH:])
    log_a = (-LRU_C * _softplus(-lam)) * r
    a = jnp.exp(log_a)
    th = jnp.tanh(log_a)
    b = jnp.sqrt(-2.0 * th / (1.0 - th)) * (i * xc)
    return a, b


def _lru_prompt_kernel(xl_ref, gate_ref, cw_ref, cb_ref, wg_ref, bg_ref, lam_ref,
                       m_ref, hlast_ref, h_ref, tail_ref, *, first_row):
    blk = pl.program_id(1)

    @pl.when(blk == 0)
    def _():
        h_ref[...] = jnp.zeros_like(h_ref)
        tail_ref[...] = jnp.zeros_like(tail_ref)

    xl = xl_ref[0]
    ext = jnp.concatenate([tail_ref[...], xl], axis=0)
    tail_ref[...] = xl[ROWS - SUBLANES:]
    xc = cb_ref[...] + jnp.zeros_like(xl)
    for j in range(CONV_WIDTH):
        shift = CONV_WIDTH - 1 - j
        shifted = xl if shift == 0 else pltpu.roll(ext, shift, 0)[SUBLANES:]
        xc = xc + shifted * cw_ref[j:j + 1, :]

    a, b = _lru_coeffs(xc, wg_ref[...], bg_ref[...], lam_ref[...])
    row = lax.broadcasted_iota(jnp.int32, (ROWS, LRU_WIDTH), 0)
    b = jnp.where(blk * ROWS + row >= first_row, b, 0.0)

    sub = row & (SUBLANES - 1)
    for s in (1, 2, 4):
        keep = sub >= s
        b = jnp.where(keep, a * pltpu.roll(b, s, 0) + b, b)
        a = jnp.where(keep, a * pltpu.roll(a, s, 0), a)
    h = h_ref[...]
    hs = []
    for g in range(ROWS // SUBLANES):
        grp = slice(g * SUBLANES, (g + 1) * SUBLANES)
        hg = a[grp] * h + b[grp]
        hs.append(hg)
        h = hg[SUBLANES - 1:]
    h_ref[...] = h
    hlast_ref[0] = h
    m_ref[0] = (jnp.concatenate(hs, axis=0) * _gelu_tanh(gate_ref[0])).astype(BF16)


def _lru_prompt(xl, gate, conv_w, conv_b, wg, bg, lam, first_row):
    batch, t_pad, _ = xl.shape
    row_spec = pl.BlockSpec((1, ROWS, LRU_WIDTH), lambda b, i: (b, i, 0))
    return pl.pallas_call(
        functools.partial(_lru_prompt_kernel, first_row=first_row),
        grid=(batch, t_pad // ROWS),
        in_specs=[
            row_spec, row_spec,
            _const_spec((CONV_WIDTH, LRU_WIDTH)), _const_spec((1, LRU_WIDTH)),
            _const_spec((LRU_WIDTH, 2 * LRU_WIDTH)), _const_spec((1, 2 * LRU_WIDTH)),
            _const_spec((1, LRU_WIDTH)),
        ],
        out_specs=[row_spec, pl.BlockSpec((1, 1, LRU_WIDTH), lambda b, i: (b, 0, 0))],
        out_shape=[jax.ShapeDtypeStruct((batch, t_pad, LRU_WIDTH), BF16),
                   jax.ShapeDtypeStruct((batch, 1, LRU_WIDTH), F32)],
        scratch_shapes=[pltpu.VMEM((1, LRU_WIDTH), F32), pltpu.VMEM((SUBLANES, LRU_WIDTH), F32)],
        compiler_params=pltpu.CompilerParams(
            dimension_semantics=("arbitrary", "arbitrary"), vmem_limit_bytes=32 * MIB),
        name="lru_prompt",
    )(xl, gate, conv_w, conv_b, wg, bg, lam)


def _lru_sample_kernel(xl_ref, gate_ref, prev_ref, h0_ref, cw_ref, cb_ref, wg_ref, bg_ref, lam_ref,
                       m_ref, hlast_ref):
    steps = xl_ref.shape[0]
    xs = [prev_ref[j] for j in range(CONV_WIDTH - 1)] + [xl_ref[t] for t in range(steps)]
    xcs = []
    for t in range(steps):
        xc = cb_ref[...] + jnp.zeros_like(xs[0])
        for j in range(CONV_WIDTH):
            xc = xc + xs[t + j] * cw_ref[j:j + 1, :]
        xcs.append(xc)
    a, b = _lru_coeffs(jnp.concatenate(xcs, axis=0), wg_ref[...], bg_ref[...], lam_ref[...])
    n = xs[0].shape[0]
    h = h0_ref[...]
    for t in range(steps):
        h = a[t * n:(t + 1) * n] * h + b[t * n:(t + 1) * n]
        m_ref[t] = (h * _gelu_tanh(gate_ref[t])).astype(BF16)
    hlast_ref[...] = h


def _lru_sample(xl_t, gate_t, prev_t, h0, conv_w, conv_b, wg, bg, lam):
    steps, n, _ = xl_t.shape
    return pl.pallas_call(
        _lru_sample_kernel,
        out_shape=[jax.ShapeDtypeStruct((steps, n, LRU_WIDTH), BF16),
                   jax.ShapeDtypeStruct((n, LRU_WIDTH), F32)],
        compiler_params=pltpu.CompilerParams(vmem_limit_bytes=32 * MIB),
        name="lru_sample",
    )(xl_t, gate_t, prev_t, h0, conv_w, conv_b, wg, bg, lam)


def _sb_weights(z, carry, u, mask):
    sp = _softplus(z)
    if mask is not None:
        sp = jnp.where(mask, sp, 0.0)
    hi = sp.astype(BF16)
    lo = (sp - hi.astype(F32)).astype(BF16)
    later = jnp.dot(hi, u, preferred_element_type=F32) + jnp.dot(lo, u, preferred_element_type=F32)
    w = jnp.exp(z - sp - later - carry)
    if mask is not None:
        w = jnp.where(mask, w, 0.0)
    return w.astype(BF16), carry + later[:, :1] + sp[:, :1]


def _attn_prompt_kernel(bias_ref, q_ref, kt_ref, v_ref, u_ref, o_ref, acc_ref, carry_ref, *, first_row):
    pair = pl.program_id(1)
    qi = pl.program_id(2)
    q2 = q_ref[0]
    lane = lax.broadcasted_iota(jnp.int32, (ROWS, HEAD_PAIR), 1)
    q_heads = (jnp.where(lane < HEAD_DIM, q2, jnp.zeros_like(q2)),
               jnp.where(lane >= HEAD_DIM, q2, jnp.zeros_like(q2)))
    u = u_ref[...]
    acc_ref[...] = jnp.zeros_like(acc_ref)
    carry_ref[...] = jnp.zeros_like(carry_ref)

    def tile(kj, masked):
        start = pl.multiple_of(kj * ROWS, ROWS)
        kt = kt_ref[0, :, pl.ds(start, ROWS)]
        vv = v_ref[0, pl.ds(start, ROWS), :]
        mask = None
        if masked:
            q_pos = qi * ROWS + lax.broadcasted_iota(jnp.int32, (ROWS, ROWS), 0)
            k_pos = kj * ROWS + lax.broadcasted_iota(jnp.int32, (ROWS, ROWS), 1)
            mask = (k_pos >= first_row) & (k_pos < q_pos)
        for e in range(2):
            z = jnp.dot(q_heads[e], kt, preferred_element_type=F32) + bias_ref[2 * pair + e]
            w, carry = _sb_weights(z, carry_ref[e], u, mask)
            carry_ref[e] = carry
            acc_ref[e] += jnp.dot(w, vv, preferred_element_type=F32)

    tile(qi, True)

    def interior(t, _):
        tile(qi - 1 - t, False)
        return 0

    lax.fori_loop(0, qi - 1, interior, 0)

    @pl.when(qi > 0)
    def _():
        tile(0, True)

    o_ref[0] = jnp.where(lane < HEAD_DIM, acc_ref[0], acc_ref[1]).astype(BF16)


def _attn_prompt(sb_bias, q, kt, v, u, first_row):
    batch, t_pad, _ = q.shape
    n_blk = t_pad // ROWS
    return pl.pallas_call(
        functools.partial(_attn_prompt_kernel, first_row=first_row),
        grid=(batch, N_HEADS // 2, n_blk),
        in_specs=[
            pl.BlockSpec(memory_space=pltpu.SMEM),
            pl.BlockSpec((1, ROWS, HEAD_PAIR), lambda b, p, i: (b, i, p)),
            pl.BlockSpec((1, HEAD_PAIR, t_pad), lambda b, p, i: (b, p, 0)),
            pl.BlockSpec((1, t_pad, HEAD_PAIR), lambda b, p, i: (b, 0, p)),
            _const_spec((ROWS, ROWS)),
        ],
        out_specs=pl.BlockSpec((1, ROWS, HEAD_PAIR), lambda b, p, i: (b, i, p)),
        out_shape=jax.ShapeDtypeStruct((batch, t_pad, ATTN_WIDTH), BF16),
        scratch_shapes=[pltpu.VMEM((2, ROWS, HEAD_PAIR), F32), pltpu.VMEM((2, ROWS, 1), F32)],
        compiler_params=pltpu.CompilerParams(
            dimension_semantics=("arbitrary", "arbitrary", "arbitrary"), vmem_limit_bytes=40 * MIB),
        name="attn_prompt",
    )(sb_bias, q, kt, v, u)


def _attn_sample_kernel(pt_ref, bias_ref, q_ref, kn_ref, vn_ref, u_ref, *refs, n_pages, page):
    del pt_ref
    k_refs = refs[:n_pages]
    v_refs = refs[n_pages:2 * n_pages]
    o_ref = refs[2 * n_pages]
    steps = q_ref.shape[1]
    rows = N_HEADS * steps

    q_all = jnp.concatenate([q_ref[0].astype(F32)] * N_HEADS, axis=0)
    row_head = lax.broadcasted_iota(jnp.int32, (rows, ATTN_WIDTH), 0) // steps
    lane_head = lax.broadcasted_iota(jnp.int32, (rows, ATTN_WIDTH), 1) // HEAD_DIM
    q_bd = jnp.where(row_head == lane_head, q_all, 0.0).astype(BF16)

    krow = lax.broadcasted_iota(jnp.int32, (rows, page), 0)
    bias = jnp.zeros((rows, page), F32)
    for h in range(N_HEADS):
        bias = jnp.where(krow // steps == h, bias_ref[h], bias)
    u = u_ref[...]

    def scores(k_tile):
        return lax.dot_general(q_bd, k_tile.astype(BF16), (((1,), (1,)), ((), ())),
                               preferred_element_type=F32) + bias

    pad = jnp.zeros((page - steps, ATTN_WIDTH), F32)
    k_new = jnp.concatenate([kn_ref[0], pad], axis=0)
    v_new = jnp.concatenate([vn_ref[0], pad], axis=0)
    mask = lax.broadcasted_iota(jnp.int32, (rows, page), 1) < krow % steps
    w, carry = _sb_weights(scores(k_new), jnp.zeros((rows, 1), F32), u, mask)
    acc = jnp.dot(w, v_new.astype(BF16), preferred_element_type=F32)
    for p in reversed(range(n_pages)):
        w, carry = _sb_weights(scores(k_refs[p][0]), carry, u, None)
        acc = acc + jnp.dot(w, v_refs[p][0].astype(BF16), preferred_element_type=F32)

    out_lane_head = lax.broadcasted_iota(jnp.int32, (steps, ATTN_WIDTH), 1) // HEAD_DIM
    out = jnp.zeros((steps, ATTN_WIDTH), F32)
    for h in range(N_HEADS):
        out = jnp.where(out_lane_head == h, acc[h * steps:(h + 1) * steps], out)
    o_ref[0] = out


def _attn_sample(page_table, sb_bias, q, k_new, v_new, u, cache_k, cache_v):
    n, steps, _ = q.shape
    n_pages = page_table.shape[1]
    page = cache_k.shape[1]
    tok_spec = pl.BlockSpec((1, steps, ATTN_WIDTH), lambda b, pt: (b, 0, 0))

    def page_spec(p):
        return pl.BlockSpec((1, page, ATTN_WIDTH), lambda b, pt: (pt[b, p], 0, 0))

    grid_spec = pltpu.PrefetchScalarGridSpec(
        num_scalar_prefetch=1,
        grid=(n,),
        in_specs=[pl.BlockSpec(memory_space=pltpu.SMEM), tok_spec, tok_spec, tok_spec,
                  pl.BlockSpec((page, page), lambda b, pt: (0, 0))]
        + [page_spec(p) for p in range(n_pages)] * 2,
        out_specs=tok_spec,
    )
    return pl.pallas_call(
        functools.partial(_attn_sample_kernel, n_pages=n_pages, page=page),
        grid_spec=grid_spec,
        out_shape=jax.ShapeDtypeStruct((n, steps, ATTN_WIDTH), F32),
        compiler_params=pltpu.CompilerParams(
            dimension_semantics=("arbitrary",), vmem_limit_bytes=48 * MIB),
        name="attn_sample",
    )(page_table, sb_bias, q, k_new, v_new, u, *([cache_k] * n_pages), *([cache_v] * n_pages))


def _outproj_mlp_kernel(attn_ref, lru_ref, x_ref, wo_ref, gpost_ref, gpre_ref, gout_ref, wu_ref, wd_ref,
                        y_ref):
    mixed = (jnp.dot(attn_ref[0].astype(BF16), wo_ref[:ATTN_WIDTH, :], preferred_element_type=F32)
             + jnp.dot(lru_ref[0], wo_ref[ATTN_WIDTH:, :], preferred_element_type=F32))
    x = x_ref[0] + _rmsnorm(mixed, gpost_ref[...])
    hn = _rmsnorm(x, gpre_ref[...]).astype(BF16)
    acc = jnp.zeros_like(x)
    for c in range(D_FF // FF_CHUNK):
        cols = slice(c * FF_CHUNK, (c + 1) * FF_CHUNK)
        up = jnp.maximum(jnp.dot(hn, wu_ref[:, cols], preferred_element_type=F32), 0.0)
        acc = acc + jnp.dot((up * up).astype(BF16), wd_ref[cols, :], preferred_element_type=F32)
    y_ref[0] = x + _rmsnorm(acc, gout_ref[...])


def _outproj_mlp(attn, lru, x, w_out, g_post, g_pre, g_out, w_up, w_down, skip_blocks):
    batch, t, _ = x.shape
    mix_spec = pl.BlockSpec((1, ROWS, ATTN_WIDTH), lambda b, i: (b, i + skip_blocks, 0))
    x_spec = pl.BlockSpec((1, ROWS, D_MODEL), lambda b, i: (b, i, 0))

    def weight_spec(shape):
        return pl.BlockSpec(shape, lambda b, i: (0, 0), pipeline_mode=pl.Buffered(1))

    return pl.pallas_call(
        _outproj_mlp_kernel,
        grid=(batch, t // ROWS),
        in_specs=[
            mix_spec, mix_spec, x_spec,
            weight_spec((D_MODEL, D_MODEL)),
            _const_spec((1, D_MODEL)), _const_spec((1, D_MODEL)), _const_spec((1, D_MODEL)),
            weight_spec((D_MODEL, D_FF)), weight_spec((D_FF, D_MODEL)),
        ],
        out_specs=x_spec,
        out_shape=jax.ShapeDtypeStruct((batch, t, D_MODEL), F32),
        compiler_params=pltpu.CompilerParams(
            dimension_semantics=("arbitrary", "arbitrary"), vmem_limit_bytes=56 * MIB),
        name="outproj_mlp",
    )(attn, lru, x, w_out, g_post, g_pre, g_out, w_up, w_down)


def _block_diag(w):
    n, c, d = w.shape
    eye = jnp.eye(n, dtype=w.dtype)
    return (eye[:, None, :, None] * w[:, :, None, :]).reshape(n * c, n * d)


def _later_keys_matrix(n):
    idx = jnp.arange(n)
    return (idx[:, None] > idx[None, :]).astype(BF16)


def kernel(x_prompt, x_sample, cache_k, cache_v, state_h, state_conv, page_table, meta_tokens,
           g_mix_pre, g_mix_post, g_mlp_pre, g_mlp_post, w_in, sb_bias, conv_w, conv_b, w_gate_a, b_gate_a,
           w_gate_x, b_gate_x, lru_lambda, w_out, w_up, w_down):
    depth = w_in.shape[0]
    assert depth == 1, "single-layer stack"
    batch, seq, _ = x_prompt.shape
    n_dec, dec_seq, _ = x_sample.shape
    assert seq % ROWS == 0 and (n_dec * dec_seq) % ROWS == 0 and dec_seq >= CONV_WIDTH - 1

    first_row = ROWS - N_META
    n_blk = seq // ROWS + 1
    t_pad = n_blk * ROWS
    meta_blk = jnp.zeros((ROWS, D_MODEL), F32).at[first_row:].set(meta_tokens.astype(F32))

    row = lambda a: a.reshape(1, -1)
    w_in_b = w_in[0].astype(BF16)
    w_out_b = w_out[0].astype(BF16)
    w_up_b = w_up[0].astype(BF16)
    w_down_b = w_down[0].astype(BF16)
    wg = jnp.concatenate([_block_diag(w_gate_a[0]), _block_diag(w_gate_x[0])], axis=1).astype(BF16)
    bg = jnp.concatenate([b_gate_a[0], b_gate_x[0]]).reshape(1, -1)
    lru_args = (conv_w[0], row(conv_b[0]), wg, bg, row(lru_lambda[0]))
    mlp_args = (w_out_b, row(g_mix_post[0]), row(g_mlp_pre[0]), row(g_mlp_post[0]), w_up_b, w_down_b)
    bias = sb_bias[0].astype(F32)

    q_p, kt_p, vb_p, k_p, v_p, xl_p, gate_p = _inproj_prompt(x_prompt, meta_blk, row(g_mix_pre[0]), w_in_b, n_blk)
    m_p, h_p = _lru_prompt(xl_p, gate_p, *lru_args, first_row)
    attn_p = _attn_prompt(bias, q_p, kt_p, vb_p, _later_keys_matrix(ROWS), first_row)
    y_prompt = _outproj_mlp(attn_p, m_p, x_prompt, *mlp_args, skip_blocks=1)

    xs = x_sample.reshape(n_dec * dec_seq, D_MODEL)
    q_s, k_s, v_s, xl_s, gate_s = _inproj_sample(xs, row(g_mix_pre[0]), w_in_b)
    by_batch = lambda a: a.reshape(n_dec, dec_seq, -1)
    time_major = lambda a: by_batch(a).swapaxes(0, 1)
    m_s, h_s = _lru_sample(time_major(xl_s), time_major(gate_s), state_conv[0].swapaxes(0, 1), state_h[0],
                           *lru_args)
    n_pool, page = cache_k.shape[1], cache_k.shape[2]
    attn_s = _attn_sample(page_table, bias, by_batch(q_s), by_batch(k_s), by_batch(v_s),
                          _later_keys_matrix(page),
                          cache_k[0].reshape(n_pool, page, ATTN_WIDTH),
                          cache_v[0].reshape(n_pool, page, ATTN_WIDTH))
    y_sample = _outproj_mlp(attn_s.reshape(1, n_dec * dec_seq, ATTN_WIDTH),
                            m_s.swapaxes(0, 1).reshape(1, n_dec * dec_seq, LRU_WIDTH),
                            xs[None], *mlp_args, skip_blocks=0)[0].reshape(x_sample.shape)

    heads = lambda a, lead: a.reshape(*lead, N_HEADS, HEAD_DIM)
    tail = CONV_WIDTH - 1
    return (
        y_prompt,
        y_sample,
        heads(k_p[:, first_row:], (1, batch, seq + N_META)),
        heads(v_p[:, first_row:], (1, batch, seq + N_META)),
        h_p.reshape(1, batch, LRU_WIDTH),
        xl_p[:, t_pad - tail:][None],
        heads(k_s, (1, n_dec, dec_seq)),
        heads(v_s, (1, n_dec, dec_seq)),
        h_s[None],
        by_batch(xl_s)[:, dec_seq - tail:][None],
    )
```

```python
import functools

import jax
import jax.numpy as jnp
from jax import lax
from jax.experimental import pallas as pl
from jax.experimental.pallas import tpu as pltpu

F32 = jnp.float32
BF16 = jnp.bfloat16

D_MODEL = 1024
N_META = 16
ATTN_WIDTH = 512
LRU_WIDTH = 512
HEAD_DIM = 64
N_HEADS = 8
LRU_BLOCKS = 8
LRU_C = 8.0
CONV_WIDTH = 4
D_FF = 4096
RMS_EPS = 1e-6
PROJ_WIDTH = 3 * ATTN_WIDTH + 2 * LRU_WIDTH

SUBLANES = 8
LANES = 128
ROWS = 256
HEAD_PAIR = 2 * HEAD_DIM
HEAD_SLOT = 128
BIAS_PIECES = 3
LOG2E = 1.4426950408889634
Q_SCALE = HEAD_DIM ** -0.5 * LOG2E
Z2_MAX = 126.0
FF_CHUNK = 1024
MIB = 1024 * 1024


def _rmsnorm(x, g):
    ms = jnp.mean(x * x, axis=-1, keepdims=True)
    return x * lax.rsqrt(ms + RMS_EPS) * g


def _gelu_tanh(x):
    return x * (0.5 * (1.0 + jnp.tanh(0.7978845608028654 * (x + 0.044715 * (x * x * x)))))


def _sigmoid(x):
    return 1.0 / (1.0 + jnp.exp(-x))


def _softplus(x):
    return jnp.maximum(x, 0.0) + jnp.log(1.0 + jnp.exp(-jnp.abs(x)))


def _inproj(x, g, w):
    hn = _rmsnorm(x, g).astype(BF16)
    return jnp.dot(hn, w, preferred_element_type=F32)


def _pad_heads(x):
    zeros = jnp.zeros((x.shape[0], HEAD_SLOT - HEAD_DIM), x.dtype)
    pieces = []
    for h in range(N_HEADS):
        pieces += [x[:, h * HEAD_DIM:(h + 1) * HEAD_DIM], zeros]
    return jnp.concatenate(pieces, axis=1)


def _inproj_prompt_kernel(x_ref, meta_ref, g_ref, w_ref, qx_ref, kx_ref,
                          q_ref, kt_ref, vb_ref, k_ref, v_ref, xl_ref, gate_ref):
    x = jnp.where(pl.program_id(1) == 0, meta_ref[...], x_ref[0])
    proj = _inproj(x, g_ref[...], w_ref[...])
    k = proj[:, ATTN_WIDTH:2 * ATTN_WIDTH]
    v = proj[:, 2 * ATTN_WIDTH:3 * ATTN_WIDTH]
    q_ref[0] = (_pad_heads(proj[:, :ATTN_WIDTH] * Q_SCALE) + qx_ref[...]).astype(BF16)
    kt_ref[0] = (_pad_heads(k) + kx_ref[...]).T.astype(BF16)
    vb_ref[0] = v.astype(BF16)
    k_ref[0] = k
    v_ref[0] = v
    xl_ref[0] = proj[:, 3 * ATTN_WIDTH:3 * ATTN_WIDTH + LRU_WIDTH]
    gate_ref[0] = proj[:, 3 * ATTN_WIDTH + LRU_WIDTH:]


def _inproj_sample_kernel(x_ref, g_ref, w_ref, q_ref, k_ref, v_ref, xl_ref, gate_ref):
    proj = _inproj(x_ref[...], g_ref[...], w_ref[...])
    q_ref[...] = (proj[:, :ATTN_WIDTH] * Q_SCALE).astype(BF16)
    k_ref[...] = proj[:, ATTN_WIDTH:2 * ATTN_WIDTH]
    v_ref[...] = proj[:, 2 * ATTN_WIDTH:3 * ATTN_WIDTH]
    xl_ref[...] = proj[:, 3 * ATTN_WIDTH:3 * ATTN_WIDTH + LRU_WIDTH]
    gate_ref[...] = proj[:, 3 * ATTN_WIDTH + LRU_WIDTH:]


def _const_spec(shape):
    return pl.BlockSpec(shape, lambda *_: (0,) * len(shape))


def _inproj_prompt(x_prompt, meta_blk, g, w_in, q_extra, k_extra, n_blk):
    batch, _, _ = x_prompt.shape
    t_pad = n_blk * ROWS
    slots = N_HEADS * HEAD_SLOT

    def rows(width, dtype):
        return jax.ShapeDtypeStruct((batch, t_pad, width), dtype)

    row_spec = pl.BlockSpec((1, ROWS, ATTN_WIDTH), lambda b, i: (b, i, 0))
    return pl.pallas_call(
        _inproj_prompt_kernel,
        grid=(batch, n_blk),
        in_specs=[
            pl.BlockSpec((1, ROWS, D_MODEL), lambda b, i: (b, jnp.maximum(i - 1, 0), 0)),
            _const_spec((ROWS, D_MODEL)),
            _const_spec((1, D_MODEL)),
            _const_spec((D_MODEL, PROJ_WIDTH)),
            _const_spec((1, slots)),
            _const_spec((1, slots)),
        ],
        out_specs=[
            pl.BlockSpec((1, ROWS, slots), lambda b, i: (b, i, 0)),
            pl.BlockSpec((1, slots, ROWS), lambda b, i: (b, 0, i)),
            row_spec, row_spec, row_spec, row_spec, row_spec,
        ],
        out_shape=[
            rows(slots, BF16),
            jax.ShapeDtypeStruct((batch, slots, t_pad), BF16),
            rows(ATTN_WIDTH, BF16), rows(ATTN_WIDTH, F32), rows(ATTN_WIDTH, F32),
            rows(LRU_WIDTH, F32), rows(LRU_WIDTH, F32),
        ],
        compiler_params=pltpu.CompilerParams(
            dimension_semantics=("arbitrary", "arbitrary"), vmem_limit_bytes=48 * MIB),
        name="inproj_prompt",
    )(x_prompt, meta_blk, g, w_in, q_extra, k_extra)


def _inproj_sample(x, g, w_in):
    n = x.shape[0]
    row_spec = pl.BlockSpec((ROWS, ATTN_WIDTH), lambda i: (i, 0))
    f32_rows = jax.ShapeDtypeStruct((n, ATTN_WIDTH), F32)
    return pl.pallas_call(
        _inproj_sample_kernel,
        grid=(n // ROWS,),
        in_specs=[
            pl.BlockSpec((ROWS, D_MODEL), lambda i: (i, 0)),
            _const_spec((1, D_MODEL)),
            _const_spec((D_MODEL, PROJ_WIDTH)),
        ],
        out_specs=[row_spec] * 5,
        out_shape=[jax.ShapeDtypeStruct((n, ATTN_WIDTH), BF16), f32_rows, f32_rows, f32_rows, f32_rows],
        compiler_params=pltpu.CompilerParams(
            dimension_semantics=("arbitrary",), vmem_limit_bytes=48 * MIB),
        name="inproj_sample",
    )(x, g, w_in)


def _lru_coeffs(xc, wg, bg, lam):
    gates = jnp.dot(xc.astype(BF16), wg, preferred_element_type=F32) + bg
    r = _sigmoid(gates[:, :LRU_WIDTH])
    i = _sigmoid(gates[:, LRU_WIDTH:])
    log_a = (-LRU_C * _softplus(-lam)) * r
    a = jnp.exp(log_a)
    th = jnp.tanh(log_a)
    b = jnp.sqrt(-2.0 * th / (1.0 - th)) * (i * xc)
    return a, b


def _lru_prompt_kernel(xl_ref, gate_ref, cw_ref, cb_ref, wg_ref, bg_ref, lam_ref,
                       m_ref, hlast_ref, h_ref, tail_ref, *, first_row):
    blk = pl.program_id(1)

    @pl.when(blk == 0)
    def _():
        h_ref[...] = jnp.zeros_like(h_ref)
        tail_ref[...] = jnp.zeros_like(tail_ref)

    xl = xl_ref[0]
    ext = jnp.concatenate([tail_ref[...], xl], axis=0)
    tail_ref[...] = xl[ROWS - SUBLANES:]
    xc = cb_ref[...] + jnp.zeros_like(xl)
    for j in range(CONV_WIDTH):
        shift = CONV_WIDTH - 1 - j
        shifted = xl if shift == 0 else pltpu.roll(ext, shift, 0)[SUBLANES:]
        xc = xc + shifted * cw_ref[j:j + 1, :]

    a, b = _lru_coeffs(xc, wg_ref[...], bg_ref[...], lam_ref[...])
    row = lax.broadcasted_iota(jnp.int32, (ROWS, LRU_WIDTH), 0)
    b = jnp.where(blk * ROWS + row >= first_row, b, 0.0)

    sub = row & (SUBLANES - 1)
    for s in (1, 2, 4):
        keep = sub >= s
        b = jnp.where(keep, a * pltpu.roll(b, s, 0) + b, b)
        a = jnp.where(keep, a * pltpu.roll(a, s, 0), a)
    h = h_ref[...]
    hs = []
    for g in range(ROWS // SUBLANES):
        grp = slice(g * SUBLANES, (g + 1) * SUBLANES)
        hg = a[grp] * h + b[grp]
        hs.append(hg)
        h = hg[SUBLANES - 1:]
    h_ref[...] = h
    hlast_ref[0] = h
    m_ref[0] = (jnp.concatenate(hs, axis=0) * _gelu_tanh(gate_ref[0])).astype(BF16)


def _lru_prompt(xl, gate, conv_w, conv_b, wg, bg, lam, first_row):
    batch, t_pad, _ = xl.shape
    row_spec = pl.BlockSpec((1, ROWS, LRU_WIDTH), lambda b, i: (b, i, 0))
    return pl.pallas_call(
        functools.partial(_lru_prompt_kernel, first_row=first_row),
        grid=(batch, t_pad // ROWS),
        in_specs=[
            row_spec, row_spec,
            _const_spec((CONV_WIDTH, LRU_WIDTH)), _const_spec((1, LRU_WIDTH)),
            _const_spec((LRU_WIDTH, 2 * LRU_WIDTH)), _const_spec((1, 2 * LRU_WIDTH)),
            _const_spec((1, LRU_WIDTH)),
        ],
        out_specs=[row_spec, pl.BlockSpec((1, 1, LRU_WIDTH), lambda b, i: (b, 0, 0))],
        out_shape=[jax.ShapeDtypeStruct((batch, t_pad, LRU_WIDTH), BF16),
                   jax.ShapeDtypeStruct((batch, 1, LRU_WIDTH), F32)],
        scratch_shapes=[pltpu.VMEM((1, LRU_WIDTH), F32), pltpu.VMEM((SUBLANES, LRU_WIDTH), F32)],
        compiler_params=pltpu.CompilerParams(
            dimension_semantics=("arbitrary", "arbitrary"), vmem_limit_bytes=32 * MIB),
        name="lru_prompt",
    )(xl, gate, conv_w, conv_b, wg, bg, lam)


def _lru_sample_kernel(xl_ref, gate_ref, prev_ref, h0_ref, cw_ref, cb_ref, wg_ref, bg_ref, lam_ref,
                       m_ref, hlast_ref):
    steps = xl_ref.shape[0]
    xs = [prev_ref[j] for j in range(CONV_WIDTH - 1)] + [xl_ref[t] for t in range(steps)]
    xcs = []
    for t in range(steps):
        xc = cb_ref[...] + jnp.zeros_like(xs[0])
        for j in range(CONV_WIDTH):
            xc = xc + xs[t + j] * cw_ref[j:j + 1, :]
        xcs.append(xc)
    a, b = _lru_coeffs(jnp.concatenate(xcs, axis=0), wg_ref[...], bg_ref[...], lam_ref[...])
    n = xs[0].shape[0]
    h = h0_ref[...]
    for t in range(steps):
        h = a[t * n:(t + 1) * n] * h + b[t * n:(t + 1) * n]
        m_ref[t] = (h * _gelu_tanh(gate_ref[t])).astype(BF16)
    hlast_ref[...] = h


def _lru_sample(xl_t, gate_t, prev_t, h0, conv_w, conv_b, wg, bg, lam):
    steps, n, _ = xl_t.shape
    return pl.pallas_call(
        _lru_sample_kernel,
        out_shape=[jax.ShapeDtypeStruct((steps, n, LRU_WIDTH), BF16),
                   jax.ShapeDtypeStruct((n, LRU_WIDTH), F32)],
        compiler_params=pltpu.CompilerParams(vmem_limit_bytes=32 * MIB),
        name="lru_sample",
    )(xl_t, gate_t, prev_t, h0, conv_w, conv_b, wg, bg, lam)


def _sb_softplus(z2, mask):
    zc = lax.clamp(-Z2_MAX, z2, Z2_MAX)
    sp = jnp.log(1.0 + jnp.exp2(zc))
    return zc, sp if mask is None else jnp.where(mask, sp, 0.0)


def _sb_weight(zc, sp, later, carry, mask):
    total = sp + later + carry
    w = jnp.exp2(zc - LOG2E * total)
    return (w if mask is None else jnp.where(mask, w, 0.0)).astype(BF16), total[:, :1]


def _attn_prompt_kernel(q_ref, kt_ref, v_ref, u_ref, o_ref, acc_ref, carry_ref, *, first_row):
    qi = pl.program_id(2)
    u = u_ref[...]
    acc_ref[...] = jnp.zeros_like(acc_ref)
    carry_ref[...] = jnp.zeros_like(carry_ref)
    row = lax.broadcasted_iota(jnp.int32, (ROWS, ROWS), 0)
    col = lax.broadcasted_iota(jnp.int32, (ROWS, ROWS), 1)

    def tile_start(kj):
        return kj * ROWS if isinstance(kj, int) else pl.multiple_of(kj * ROWS, ROWS)

    def tiles(kjs, masked):
        heads = range(2)
        masks = [(kj * ROWS + col >= first_row) & (kj * ROWS + col < qi * ROWS + row) if m else None
                 for kj, m in zip(kjs, masked)]
        vv = v_ref[0, pl.ds(tile_start(kjs[-1]), len(kjs) * ROWS), :]
        carries = [carry_ref[e] for e in heads]
        accs = [acc_ref[e] for e in heads]
        zs = [[jnp.dot(q_ref[0, :, e * HEAD_SLOT:(e + 1) * HEAD_SLOT],
                       kt_ref[0, e * HEAD_SLOT:(e + 1) * HEAD_SLOT, pl.ds(tile_start(kj), ROWS)],
                       preferred_element_type=F32) for kj in kjs] for e in heads]
        sps = [[_sb_softplus(z, mask) for z, mask in zip(zs[e], masks)] for e in heads]
        laters = [[jnp.dot(sp.astype(BF16), u, preferred_element_type=F32) for _, sp in sps[e]] for e in heads]
        ws = [[] for _ in heads]
        for t, mask in enumerate(masks):
            for e in heads:
                w, carries[e] = _sb_weight(*sps[e][t], laters[e][t], carries[e], mask)
                ws[e].append(w)
        pvs = [jnp.dot(jnp.concatenate(ws[e][::-1], axis=1), vv, preferred_element_type=F32) for e in heads]
        for e in heads:
            carry_ref[e] = carries[e]
            acc_ref[e] = accs[e] + pvs[e]

    tiles([qi], [True])

    def interior_pair(t, _):
        kj = qi - 1 - 2 * t
        tiles([kj, kj - 1], [False, False])
        return 0

    n_interior = jnp.maximum(qi - 1, 0)
    lax.fori_loop(0, n_interior // 2, interior_pair, 0)

    @pl.when(n_interior % 2 == 1)
    def _():
        tiles([1, 0], [False, True])

    @pl.when(jnp.logical_and(qi > 0, n_interior % 2 == 0))
    def _():
        tiles([0], [True])

    lane = lax.broadcasted_iota(jnp.int32, (ROWS, HEAD_PAIR), 1)
    o_ref[0] = jnp.where(lane < HEAD_DIM, acc_ref[0], acc_ref[1]).astype(BF16)


def _attn_prompt(q, kt, v, u, first_row):
    batch, t_pad, _ = v.shape
    n_blk = t_pad // ROWS
    return pl.pallas_call(
        functools.partial(_attn_prompt_kernel, first_row=first_row),
        grid=(batch, N_HEADS // 2, n_blk),
        in_specs=[
            pl.BlockSpec((1, ROWS, 2 * HEAD_SLOT), lambda b, p, i: (b, i, p)),
            pl.BlockSpec((1, 2 * HEAD_SLOT, t_pad), lambda b, p, i: (b, p, 0)),
            pl.BlockSpec((1, t_pad, HEAD_PAIR), lambda b, p, i: (b, 0, p)),
            _const_spec((ROWS, ROWS)),
        ],
        out_specs=pl.BlockSpec((1, ROWS, HEAD_PAIR), lambda b, p, i: (b, i, p)),
        out_shape=jax.ShapeDtypeStruct((batch, t_pad, ATTN_WIDTH), BF16),
        scratch_shapes=[pltpu.VMEM((2, ROWS, HEAD_PAIR), F32), pltpu.VMEM((2, ROWS, 1), F32)],
        compiler_params=pltpu.CompilerParams(
            dimension_semantics=("arbitrary", "arbitrary", "arbitrary"), vmem_limit_bytes=40 * MIB),
        name="attn_prompt",
    )(q, kt, v, u)


def _attn_sample_kernel(pt_ref, bias_ref, q_ref, kn_ref, vn_ref, u_ref, *refs, n_pages):
    del pt_ref
    k_refs = refs[:n_pages]
    v_refs = refs[n_pages:2 * n_pages]
    o_ref = refs[2 * n_pages]
    steps = q_ref.shape[1] // N_HEADS
    page_lanes = k_refs[0].shape[0] * N_HEADS
    q_all = q_ref[0]
    umat = u_ref[...]

    def lane_head(width):
        return lax.broadcasted_iota(jnp.int32, (steps, width), 1) & (N_HEADS - 1)

    bias = jnp.zeros((steps, page_lanes), F32)
    for h in range(N_HEADS):
        bias = jnp.where(lane_head(page_lanes) == h, bias_ref[h], bias)

    def scores(k_rows):
        width = k_rows.shape[0]
        zz = lax.dot_general(q_all, k_rows.astype(BF16), (((1,), (1,)), ((), ())),
                             preferred_element_type=F32)
        z = zz[:steps]
        for h in range(1, N_HEADS):
            z = jnp.where(lane_head(width) == h, zz[h * steps:(h + 1) * steps], z)
        return z + bias[:, :width]

    def lane_blocks(x):
        return [x[:, j:j + LANES] for j in reversed(range(0, x.shape[1], LANES))]

    pad = jnp.zeros((LANES - steps * N_HEADS, HEAD_DIM), F32)
    new_mask = (lax.broadcasted_iota(jnp.int32, (steps, LANES), 1) // N_HEADS
                < lax.broadcasted_iota(jnp.int32, (steps, LANES), 0))
    zc_new, sp_new = _sb_softplus(scores(jnp.concatenate([kn_ref[0], pad], axis=0)), new_mask)
    zcs, sps = [zc_new], [sp_new]
    for p in reversed(range(n_pages)):
        zc, sp = _sb_softplus(scores(k_refs[p][...].reshape(page_lanes, HEAD_DIM)), None)
        zcs += lane_blocks(zc)
        sps += lane_blocks(sp)

    sums = jnp.dot(jnp.concatenate(sps, axis=0).astype(BF16), umat, preferred_element_type=F32)
    carry = jnp.zeros((steps, LANES), F32)
    ws = []
    for n, (zc, sp) in enumerate(zip(zcs, sps)):
        blk = sums[n * steps:(n + 1) * steps]
        ws.append(jnp.exp2(zc - LOG2E * (sp + blk[:, :LANES] + carry)))
        carry = carry + blk[:, LANES:]
    ws[0] = jnp.where(new_mask, ws[0], 0.0)

    def spread_heads(w):
        return jnp.concatenate([jnp.where(lane_head(w.shape[1]) == h, w, 0.0) for h in range(N_HEADS)],
                               axis=0).astype(BF16)

    acc = jnp.dot(spread_heads(ws[0]), jnp.concatenate([vn_ref[0], pad], axis=0).astype(BF16),
                  preferred_element_type=F32)
    per_page = page_lanes // LANES
    for i, p in enumerate(reversed(range(n_pages))):
        w_page = jnp.concatenate(ws[1 + i * per_page:1 + (i + 1) * per_page][::-1], axis=1)
        acc = acc + jnp.dot(spread_heads(w_page), v_refs[p][...].reshape(page_lanes, HEAD_DIM).astype(BF16),
                            preferred_element_type=F32)
    o_ref[0] = acc


def _attn_sample(page_table, bias2, q, k_new, v_new, umat, cache_k, cache_v):
    n, rows, _ = q.shape
    n_pages = page_table.shape[1]
    page = cache_k.shape[2]
    tok_spec = pl.BlockSpec((1, rows, HEAD_DIM), lambda b, pt: (b, 0, 0))

    def page_spec(p):
        return pl.BlockSpec((None, None, page, N_HEADS, HEAD_DIM), lambda b, pt: (0, pt[b, p], 0, 0, 0))

    grid_spec = pltpu.PrefetchScalarGridSpec(
        num_scalar_prefetch=1,
        grid=(n,),
        in_specs=[pl.BlockSpec(memory_space=pltpu.SMEM), tok_spec, tok_spec, tok_spec,
                  pl.BlockSpec((LANES, 2 * LANES), lambda b, pt: (0, 0))]
        + [page_spec(p) for p in range(n_pages)] * 2,
        out_specs=tok_spec,
    )
    return pl.pallas_call(
        functools.partial(_attn_sample_kernel, n_pages=n_pages),
        grid_spec=grid_spec,
        out_shape=jax.ShapeDtypeStruct((n, rows, HEAD_DIM), F32),
        compiler_params=pltpu.CompilerParams(
            dimension_semantics=("arbitrary",), vmem_limit_bytes=56 * MIB),
        name="attn_sample",
    )(page_table, bias2, q, k_new, v_new, umat, *([cache_k] * n_pages), *([cache_v] * n_pages))


def _outproj_mlp_kernel(attn_ref, lru_ref, x_ref, wo_ref, gpost_ref, gpre_ref, gout_ref, wu_ref, wd_ref,
                        y_ref):
    mixed = (jnp.dot(attn_ref[0].astype(BF16), wo_ref[:ATTN_WIDTH, :], preferred_element_type=F32)
             + jnp.dot(lru_ref[0], wo_ref[ATTN_WIDTH:, :], preferred_element_type=F32))
    x = x_ref[0] + _rmsnorm(mixed, gpost_ref[...])
    hn = _rmsnorm(x, gpre_ref[...]).astype(BF16)
    acc = jnp.zeros_like(x)
    for c in range(D_FF // FF_CHUNK):
        cols = slice(c * FF_CHUNK, (c + 1) * FF_CHUNK)
        up = jnp.maximum(jnp.dot(hn, wu_ref[:, cols], preferred_element_type=F32), 0.0)
        acc = acc + jnp.dot((up * up).astype(BF16), wd_ref[cols, :], preferred_element_type=F32)
    y_ref[0] = x + _rmsnorm(acc, gout_ref[...])


def _outproj_mlp(attn, lru, x, w_out, g_post, g_pre, g_out, w_up, w_down, skip_blocks):
    batch, t, _ = x.shape
    mix_spec = pl.BlockSpec((1, ROWS, ATTN_WIDTH), lambda b, i: (b, i + skip_blocks, 0))
    x_spec = pl.BlockSpec((1, ROWS, D_MODEL), lambda b, i: (b, i, 0))

    def weight_spec(shape):
        return pl.BlockSpec(shape, lambda b, i: (0, 0), pipeline_mode=pl.Buffered(1))

    return pl.pallas_call(
        _outproj_mlp_kernel,
        grid=(batch, t // ROWS),
        in_specs=[
            mix_spec, mix_spec, x_spec,
            weight_spec((D_MODEL, D_MODEL)),
            _const_spec((1, D_MODEL)), _const_spec((1, D_MODEL)), _const_spec((1, D_MODEL)),
            weight_spec((D_MODEL, D_FF)), weight_spec((D_FF, D_MODEL)),
        ],
        out_specs=x_spec,
        out_shape=jax.ShapeDtypeStruct((batch, t, D_MODEL), F32),
        compiler_params=pltpu.CompilerParams(
            dimension_semantics=("arbitrary", "arbitrary"), vmem_limit_bytes=56 * MIB),
        name="outproj_mlp",
    )(attn, lru, x, w_out, g_post, g_pre, g_out, w_up, w_down)


def _block_diag(w):
    n, c, d = w.shape
    eye = jnp.eye(n, dtype=w.dtype)
    return (eye[:, None, :, None] * w[:, :, None, :]).reshape(n * c, n * d)


def _later_keys_matrix(n):
    idx = jnp.arange(n)
    return (idx[:, None] > idx[None, :]).astype(BF16)


def _lane_block_sums_matrix():
    idx = jnp.arange(LANES)
    tok, head = idx // N_HEADS, idx % N_HEADS
    same_head = head[:, None] == head[None, :]
    later = same_head & (tok[:, None] > tok[None, :])
    return jnp.concatenate([later, same_head], axis=1).astype(BF16)


def _bias_lanes(bias2):
    pieces = []
    rest = bias2
    for _ in range(BIAS_PIECES):
        piece = rest.astype(BF16).astype(F32)
        pieces.append(piece)
        rest = rest - piece
    k_slot = jnp.zeros((N_HEADS, HEAD_SLOT), F32).at[:, HEAD_DIM:HEAD_DIM + BIAS_PIECES].set(
        jnp.stack(pieces, axis=1))
    q_slot = jnp.zeros((N_HEADS, HEAD_SLOT), F32).at[:, HEAD_DIM:HEAD_DIM + BIAS_PIECES].set(1.0)
    return q_slot.reshape(1, -1), k_slot.reshape(1, -1)


def kernel(x_prompt, x_sample, cache_k, cache_v, state_h, state_conv, page_table, meta_tokens,
           g_mix_pre, g_mix_post, g_mlp_pre, g_mlp_post, w_in, sb_bias, conv_w, conv_b, w_gate_a, b_gate_a,
           w_gate_x, b_gate_x, lru_lambda, w_out, w_up, w_down):
    depth = w_in.shape[0]
    assert depth == 1, "single-layer stack"
    batch, seq, _ = x_prompt.shape
    n_dec, dec_seq, _ = x_sample.shape
    assert seq % ROWS == 0 and (n_dec * dec_seq) % ROWS == 0 and dec_seq >= CONV_WIDTH - 1

    first_row = ROWS - N_META
    n_blk = seq // ROWS + 1
    t_pad = n_blk * ROWS
    meta_blk = jnp.zeros((ROWS, D_MODEL), F32).at[first_row:].set(meta_tokens.astype(F32))

    row = lambda a: a.reshape(1, -1)
    w_in_b = w_in[0].astype(BF16)
    w_out_b = w_out[0].astype(BF16)
    w_up_b = w_up[0].astype(BF16)
    w_down_b = w_down[0].astype(BF16)
    wg = jnp.concatenate([_block_diag(w_gate_a[0]), _block_diag(w_gate_x[0])], axis=1).astype(BF16)
    bg = jnp.concatenate([b_gate_a[0], b_gate_x[0]]).reshape(1, -1)
    lru_args = (conv_w[0], row(conv_b[0]), wg, bg, row(lru_lambda[0]))
    mlp_args = (w_out_b, row(g_mix_post[0]), row(g_mlp_pre[0]), row(g_mlp_post[0]), w_up_b, w_down_b)
    bias2 = sb_bias[0].astype(F32) * LOG2E
    q_extra, k_extra = _bias_lanes(bias2)

    q_p, kt_p, vb_p, k_p, v_p, xl_p, gate_p = _inproj_prompt(
        x_prompt, meta_blk, row(g_mix_pre[0]), w_in_b, q_extra, k_extra, n_blk)
    m_p, h_p = _lru_prompt(xl_p, gate_p, *lru_args, first_row)
    attn_p = _attn_prompt(q_p, kt_p, vb_p, _later_keys_matrix(ROWS), first_row)
    y_prompt = _outproj_mlp(attn_p, m_p, x_prompt, *mlp_args, skip_blocks=1)

    xs = x_sample.reshape(n_dec * dec_seq, D_MODEL)
    q_s, k_s, v_s, xl_s, gate_s = _inproj_sample(xs, row(g_mix_pre[0]), w_in_b)
    by_batch = lambda a: a.reshape(n_dec, dec_seq, -1)
    time_major = lambda a: by_batch(a).swapaxes(0, 1)
    m_s, h_s = _lru_sample(time_major(xl_s), time_major(gate_s), state_conv[0].swapaxes(0, 1), state_h[0],
                           *lru_args)
    head_major = lambda a: (a.reshape(n_dec, dec_seq, N_HEADS, HEAD_DIM).swapaxes(1, 2)
                            .reshape(n_dec, N_HEADS * dec_seq, HEAD_DIM))
    cache_order = lambda a: a.reshape(n_dec, dec_seq * N_HEADS, HEAD_DIM)
    attn_s = _attn_sample(page_table, bias2, head_major(q_s), cache_order(k_s), cache_order(v_s),
                          _lane_block_sums_matrix(), cache_k, cache_v)
    attn_s = attn_s.reshape(n_dec, N_HEADS, dec_seq, HEAD_DIM).swapaxes(1, 2)
    y_sample = _outproj_mlp(attn_s.reshape(1, n_dec * dec_seq, ATTN_WIDTH),
                            m_s.swapaxes(0, 1).reshape(1, n_dec * dec_seq, LRU_WIDTH),
                            xs[None], *mlp_args, skip_blocks=0)[0].reshape(x_sample.shape)

    heads = lambda a, lead: a.reshape(*lead, N_HEADS, HEAD_DIM)
    tail = CONV_WIDTH - 1
    return (
        y_prompt,
        y_sample,
        heads(k_p[:, first_row:], (1, batch, seq + N_META)),
        heads(v_p[:, first_row:], (1, batch, seq + N_META)),
        h_p.reshape(1, batch, LRU_WIDTH),
        xl_p[:, t_pad - tail:][None],
        heads(k_s, (1, n_dec, dec_seq)),
        heads(v_s, (1, n_dec, dec_seq)),
        h_s[None],
        by_batch(xl_s)[:, dec_seq - tail:][None],
    )
```

```python
import functools

import jax
import jax.numpy as jnp
from jax import lax
from jax.experimental import pallas as pl
from jax.experimental.pallas import tpu as pltpu

F32 = jnp.float32
BF16 = jnp.bfloat16

D_MODEL = 1024
N_META = 16
ATTN_WIDTH = 512
LRU_WIDTH = 512
HEAD_DIM = 64
N_HEADS = 8
LRU_BLOCKS = 8
LRU_C = 8.0
CONV_WIDTH = 4
D_FF = 4096
RMS_EPS = 1e-6
PROJ_WIDTH = 3 * ATTN_WIDTH + 2 * LRU_WIDTH

SUBLANES = 8
LANES = 128
ROWS = 256
HEAD_PAIR = 2 * HEAD_DIM
HEAD_SLOT = 128
BIAS_PIECES = 3
LOG2E = 1.4426950408889634
Q_SCALE = HEAD_DIM ** -0.5 * LOG2E
Z2_MAX = 126.0
FF_CHUNK = 1024
MIB = 1024 * 1024


def _rmsnorm(x, g):
    ms = jnp.mean(x * x, axis=-1, keepdims=True)
    return x * lax.rsqrt(ms + RMS_EPS) * g


def _gelu_tanh(x):
    return x * (0.5 * (1.0 + jnp.tanh(0.7978845608028654 * (x + 0.044715 * (x * x * x)))))


def _sigmoid(x):
    return 1.0 / (1.0 + jnp.exp(-x))


def _softplus(x):
    return jnp.maximum(x, 0.0) + jnp.log(1.0 + jnp.exp(-jnp.abs(x)))


def _inproj(x, g, w):
    hn = _rmsnorm(x, g).astype(BF16)
    return jnp.dot(hn, w, preferred_element_type=F32)


def _pad_heads(x):
    zeros = jnp.zeros((x.shape[0], HEAD_SLOT - HEAD_DIM), x.dtype)
    pieces = []
    for h in range(N_HEADS):
        pieces += [x[:, h * HEAD_DIM:(h + 1) * HEAD_DIM], zeros]
    return jnp.concatenate(pieces, axis=1)


def _inproj_prompt_kernel(x_ref, meta_ref, g_ref, w_ref, qx_ref, kx_ref,
                          q_ref, kt_ref, vb_ref, k_ref, v_ref, xl_ref, gate_ref):
    x = jnp.where(pl.program_id(1) == 0, meta_ref[...], x_ref[0])
    proj = _inproj(x, g_ref[...], w_ref[...])
    k = proj[:, ATTN_WIDTH:2 * ATTN_WIDTH]
    v = proj[:, 2 * ATTN_WIDTH:3 * ATTN_WIDTH]
    q_ref[0] = (_pad_heads(proj[:, :ATTN_WIDTH] * Q_SCALE) + qx_ref[...]).astype(BF16)
    kt_ref[0] = (_pad_heads(k) + kx_ref[...]).T.astype(BF16)
    vb_ref[0] = v.astype(BF16)
    k_ref[0] = k
    v_ref[0] = v
    xl_ref[0] = proj[:, 3 * ATTN_WIDTH:3 * ATTN_WIDTH + LRU_WIDTH]
    gate_ref[0] = proj[:, 3 * ATTN_WIDTH + LRU_WIDTH:]


def _inproj_sample_kernel(x_ref, g_ref, w_ref, q_ref, k_ref, v_ref, xl_ref, gate_ref):
    proj = _inproj(x_ref[...], g_ref[...], w_ref[...])
    q_ref[...] = (proj[:, :ATTN_WIDTH] * Q_SCALE).astype(BF16)
    k_ref[...] = proj[:, ATTN_WIDTH:2 * ATTN_WIDTH]
    v_ref[...] = proj[:, 2 * ATTN_WIDTH:3 * ATTN_WIDTH]
    xl_ref[...] = proj[:, 3 * ATTN_WIDTH:3 * ATTN_WIDTH + LRU_WIDTH]
    gate_ref[...] = proj[:, 3 * ATTN_WIDTH + LRU_WIDTH:]


def _const_spec(shape):
    return pl.BlockSpec(shape, lambda *_: (0,) * len(shape))


def _inproj_prompt(x_prompt, meta_blk, g, w_in, q_extra, k_extra, n_blk):
    batch, _, _ = x_prompt.shape
    t_pad = n_blk * ROWS
    slots = N_HEADS * HEAD_SLOT

    def rows(width, dtype):
        return jax.ShapeDtypeStruct((batch, t_pad, width), dtype)

    row_spec = pl.BlockSpec((1, ROWS, ATTN_WIDTH), lambda b, i: (b, i, 0))
    return pl.pallas_call(
        _inproj_prompt_kernel,
        grid=(batch, n_blk),
        in_specs=[
            pl.BlockSpec((1, ROWS, D_MODEL), lambda b, i: (b, jnp.maximum(i - 1, 0), 0)),
            _const_spec((ROWS, D_MODEL)),
            _const_spec((1, D_MODEL)),
            _const_spec((D_MODEL, PROJ_WIDTH)),
            _const_spec((1, slots)),
            _const_spec((1, slots)),
        ],
        out_specs=[
            pl.BlockSpec((1, ROWS, slots), lambda b, i: (b, i, 0)),
            pl.BlockSpec((1, slots, ROWS), lambda b, i: (b, 0, i)),
            row_spec, row_spec, row_spec, row_spec, row_spec,
        ],
        out_shape=[
            rows(slots, BF16),
            jax.ShapeDtypeStruct((batch, slots, t_pad), BF16),
            rows(ATTN_WIDTH, BF16), rows(ATTN_WIDTH, F32), rows(ATTN_WIDTH, F32),
            rows(LRU_WIDTH, F32), rows(LRU_WIDTH, F32),
        ],
        compiler_params=pltpu.CompilerParams(
            dimension_semantics=("arbitrary", "arbitrary"), vmem_limit_bytes=48 * MIB),
        name="inproj_prompt",
    )(x_prompt, meta_blk, g, w_in, q_extra, k_extra)


def _inproj_sample(x, g, w_in):
    n = x.shape[0]
    row_spec = pl.BlockSpec((ROWS, ATTN_WIDTH), lambda i: (i, 0))
    f32_rows = jax.ShapeDtypeStruct((n, ATTN_WIDTH), F32)
    return pl.pallas_call(
        _inproj_sample_kernel,
        grid=(n // ROWS,),
        in_specs=[
            pl.BlockSpec((ROWS, D_MODEL), lambda i: (i, 0)),
            _const_spec((1, D_MODEL)),
            _const_spec((D_MODEL, PROJ_WIDTH)),
        ],
        out_specs=[row_spec] * 5,
        out_shape=[jax.ShapeDtypeStruct((n, ATTN_WIDTH), BF16), f32_rows, f32_rows, f32_rows, f32_rows],
        compiler_params=pltpu.CompilerParams(
            dimension_semantics=("arbitrary",), vmem_limit_bytes=48 * MIB),
        name="inproj_sample",
    )(x, g, w_in)


def _lru_coeffs(xc, wg, bg, lam):
    gates = jnp.dot(xc.astype(BF16), wg, preferred_element_type=F32) + bg
    r = _sigmoid(gates[:, :LRU_WIDTH])
    i = _sigmoid(gates[:, LRU_WIDTH:])
    log_a = (-LRU_C * _softplus(-lam)) * r
    a = jnp.exp(log_a)
    th = jnp.tanh(log_a)
    b = jnp.sqrt(-2.0 * th / (1.0 - th)) * (i * xc)
    return a, b


def _lru_prompt_kernel(xl_ref, gate_ref, cw_ref, cb_ref, wg_ref, bg_ref, lam_ref,
                       m_ref, hlast_ref, h_ref, tail_ref, *, first_row):
    blk = pl.program_id(1)

    @pl.when(blk == 0)
    def _():
        h_ref[...] = jnp.zeros_like(h_ref)
        tail_ref[...] = jnp.zeros_like(tail_ref)

    xl = xl_ref[0]
    ext = jnp.concatenate([tail_ref[...], xl], axis=0)
    tail_ref[...] = xl[ROWS - SUBLANES:]
    xc = cb_ref[...] + jnp.zeros_like(xl)
    for j in range(CONV_WIDTH):
        shift = CONV_WIDTH - 1 - j
        shifted = xl if shift == 0 else pltpu.roll(ext, shift, 0)[SUBLANES:]
        xc = xc + shifted * cw_ref[j:j + 1, :]

    a, b = _lru_coeffs(xc, wg_ref[...], bg_ref[...], lam_ref[...])
    row = lax.broadcasted_iota(jnp.int32, (ROWS, LRU_WIDTH), 0)
    b = jnp.where(blk * ROWS + row >= first_row, b, 0.0)

    sub = row & (SUBLANES - 1)
    for s in (1, 2, 4):
        keep = sub >= s
        b = jnp.where(keep, a * pltpu.roll(b, s, 0) + b, b)
        a = jnp.where(keep, a * pltpu.roll(a, s, 0), a)
    h = h_ref[...]
    hs = []
    for g in range(ROWS // SUBLANES):
        grp = slice(g * SUBLANES, (g + 1) * SUBLANES)
        hg = a[grp] * h + b[grp]
        hs.append(hg)
        h = hg[SUBLANES - 1:]
    h_ref[...] = h
    hlast_ref[0] = h
    m_ref[0] = (jnp.concatenate(hs, axis=0) * _gelu_tanh(gate_ref[0])).astype(BF16)


def _lru_prompt(xl, gate, conv_w, conv_b, wg, bg, lam, first_row):
    batch, t_pad, _ = xl.shape
    row_spec = pl.BlockSpec((1, ROWS, LRU_WIDTH), lambda b, i: (b, i, 0))
    return pl.pallas_call(
        functools.partial(_lru_prompt_kernel, first_row=first_row),
        grid=(batch, t_pad // ROWS),
        in_specs=[
            row_spec, row_spec,
            _const_spec((CONV_WIDTH, LRU_WIDTH)), _const_spec((1, LRU_WIDTH)),
            _const_spec((LRU_WIDTH, 2 * LRU_WIDTH)), _const_spec((1, 2 * LRU_WIDTH)),
            _const_spec((1, LRU_WIDTH)),
        ],
        out_specs=[row_spec, pl.BlockSpec((1, 1, LRU_WIDTH), lambda b, i: (b, 0, 0))],
        out_shape=[jax.ShapeDtypeStruct((batch, t_pad, LRU_WIDTH), BF16),
                   jax.ShapeDtypeStruct((batch, 1, LRU_WIDTH), F32)],
        scratch_shapes=[pltpu.VMEM((1, LRU_WIDTH), F32), pltpu.VMEM((SUBLANES, LRU_WIDTH), F32)],
        compiler_params=pltpu.CompilerParams(
            dimension_semantics=("arbitrary", "arbitrary"), vmem_limit_bytes=32 * MIB),
        name="lru_prompt",
    )(xl, gate, conv_w, conv_b, wg, bg, lam)


def _lru_sample_kernel(xl_ref, gate_ref, prev_ref, h0_ref, cw_ref, cb_ref, wg_ref, bg_ref, lam_ref,
                       m_ref, hlast_ref):
    steps = xl_ref.shape[0]
    xs = [prev_ref[j] for j in range(CONV_WIDTH - 1)] + [xl_ref[t] for t in range(steps)]
    xcs = []
    for t in range(steps):
        xc = cb_ref[...] + jnp.zeros_like(xs[0])
        for j in range(CONV_WIDTH):
            xc = xc + xs[t + j] * cw_ref[j:j + 1, :]
        xcs.append(xc)
    a, b = _lru_coeffs(jnp.concatenate(xcs, axis=0), wg_ref[...], bg_ref[...], lam_ref[...])
    n = xs[0].shape[0]
    h = h0_ref[...]
    for t in range(steps):
        h = a[t * n:(t + 1) * n] * h + b[t * n:(t + 1) * n]
        m_ref[t] = (h * _gelu_tanh(gate_ref[t])).astype(BF16)
    hlast_ref[...] = h


def _lru_sample(xl_t, gate_t, prev_t, h0, conv_w, conv_b, wg, bg, lam):
    steps, n, _ = xl_t.shape
    return pl.pallas_call(
        _lru_sample_kernel,
        out_shape=[jax.ShapeDtypeStruct((steps, n, LRU_WIDTH), BF16),
                   jax.ShapeDtypeStruct((n, LRU_WIDTH), F32)],
        compiler_params=pltpu.CompilerParams(vmem_limit_bytes=32 * MIB),
        name="lru_sample",
    )(xl_t, gate_t, prev_t, h0, conv_w, conv_b, wg, bg, lam)


def _sb_clamp(z2):
    return lax.clamp(-Z2_MAX, z2, Z2_MAX)


def _sb_softplus(zc, mask):
    sp = jnp.log(1.0 + jnp.exp2(zc))
    return sp if mask is None else jnp.where(mask, sp, 0.0)


def _sb_weight(zc, sp, later, carry, mask):
    total = sp + later + carry
    w = jnp.exp2(zc - LOG2E * total)
    return (w if mask is None else jnp.where(mask, w, 0.0)).astype(BF16), total[:, :1]


def _attn_prompt_kernel(q_ref, kt_ref, v_ref, u_ref, o_ref, acc_ref, carry_ref, z_ref, w_ref, state_ref, *,
                        first_row):
    qi = pl.program_id(2)
    heads = range(2)
    u = u_ref[...]
    acc_ref[...] = jnp.zeros_like(acc_ref)
    carry_ref[...] = jnp.zeros_like(carry_ref)
    w_ref[0] = jnp.zeros(w_ref.shape[1:], w_ref.dtype)
    state_ref[0] = 0
    state_ref[1] = 0
    row = lax.broadcasted_iota(jnp.int32, (ROWS, ROWS), 0)
    col = lax.broadcasted_iota(jnp.int32, (ROWS, ROWS), 1)

    def tile_start(kj):
        return kj * ROWS if isinstance(kj, int) else pl.multiple_of(kj * ROWS, ROWS)

    def logits(e, kj):
        slot = slice(e * HEAD_SLOT, (e + 1) * HEAD_SLOT)
        return _sb_clamp(jnp.dot(q_ref[0, :, slot], kt_ref[0, slot, pl.ds(tile_start(kj), ROWS)],
                                 preferred_element_type=F32))

    def step(kjs, masked, prefetch):
        masks = [(kj * ROWS + col >= first_row) & (kj * ROWS + col < qi * ROWS + row) if m else None
                 for kj, m in zip(kjs, masked)]
        cur = state_ref[1]
        nxt = 1 - cur
        carries = [carry_ref[e] for e in heads]
        accs = [acc_ref[e] for e in heads]
        zs = [[z_ref[cur, e, s] for s in range(len(kjs))] for e in heads]
        w_prev = [w_ref[cur, e] for e in heads]
        v_prev = v_ref[0, pl.ds(pl.multiple_of(state_ref[0] * ROWS, ROWS), 2 * ROWS), :]

        pvs = [None for _ in heads]
        z_next = [[None, None] for _ in heads]

        def pv_job(e):
            def run():
                pvs[e] = jnp.dot(w_prev[e], v_prev, preferred_element_type=F32)
            return run

        def z_job(e, s):
            def run():
                z_next[e][s] = logits(e, jnp.maximum(kjs[-1] - 1 - s, 0))
            return run

        fill = [pv_job(e) for e in heads]
        if prefetch:
            fill += [z_job(e, s) for s in range(2) for e in heads]
        chains = [(t, e) for t in range(len(kjs)) for e in heads]
        sps = [[None] * len(kjs) for _ in heads]
        laters = [[None] * len(kjs) for _ in heads]
        for n, (t, e) in enumerate(chains):
            sps[e][t] = _sb_softplus(zs[e][t], masks[t])
            laters[e][t] = jnp.dot(sps[e][t].astype(BF16), u, preferred_element_type=F32)
            for job in fill[n * len(fill) // len(chains):(n + 1) * len(fill) // len(chains)]:
                job()
        ws = [[] for _ in heads]
        for t, mask in enumerate(masks):
            for e in heads:
                w, carries[e] = _sb_weight(zs[e][t], sps[e][t], laters[e][t], carries[e], mask)
                ws[e].append(w)
        for e in heads:
            carry_ref[e] = carries[e]
            if len(kjs) == 2:
                acc_ref[e] = accs[e] + pvs[e]
                w_ref[nxt, e] = jnp.concatenate(ws[e][::-1], axis=1)
            else:
                own = jnp.dot(ws[e][0], v_ref[0, pl.ds(tile_start(kjs[0]), ROWS), :], preferred_element_type=F32)
                acc_ref[e] = accs[e] + pvs[e] + own
            if prefetch:
                for s in range(2):
                    z_ref[nxt, e, s] = z_next[e][s]
        if len(kjs) == 2:
            state_ref[0] = jnp.asarray(kjs[-1], jnp.int32)
            state_ref[1] = nxt

    for e in heads:
        for s, kj in enumerate([qi, jnp.maximum(qi - 1, 0)]):
            z_ref[0, e, s] = logits(e, kj)

    odd = qi % 2
    n_plain_pairs = (qi + 1) // 2 - 1 - odd

    @pl.when(qi >= 1)
    def _():
        step([qi, qi - 1], [True, True], prefetch=True)

    def plain_pair(t, _):
        kj = qi - 2 * (t + 1)
        step([kj, kj - 1], [False, False], prefetch=True)
        return 0

    lax.fori_loop(0, n_plain_pairs, plain_pair, 0)

    @pl.when(jnp.logical_and(odd == 1, qi >= 3))
    def _():
        step([1, 0], [False, True], prefetch=False)

    @pl.when(odd == 0)
    def _():
        step([0], [True], prefetch=False)

    @pl.when(odd == 1)
    def _():
        for e in heads:
            acc_ref[e] += jnp.dot(w_ref[state_ref[1], e], v_ref[0, 0:2 * ROWS, :], preferred_element_type=F32)

    lane = lax.broadcasted_iota(jnp.int32, (ROWS, HEAD_PAIR), 1)
    o_ref[0] = jnp.where(lane < HEAD_DIM, acc_ref[0], acc_ref[1]).astype(BF16)


def _attn_prompt(q, kt, v, u, first_row):
    batch, t_pad, _ = v.shape
    n_blk = t_pad // ROWS
    return pl.pallas_call(
        functools.partial(_attn_prompt_kernel, first_row=first_row),
        grid=(batch, N_HEADS // 2, n_blk),
        in_specs=[
            pl.BlockSpec((1, ROWS, 2 * HEAD_SLOT), lambda b, p, i: (b, i, p)),
            pl.BlockSpec((1, 2 * HEAD_SLOT, t_pad), lambda b, p, i: (b, p, 0)),
            pl.BlockSpec((1, t_pad, HEAD_PAIR), lambda b, p, i: (b, 0, p)),
            _const_spec((ROWS, ROWS)),
        ],
        out_specs=pl.BlockSpec((1, ROWS, HEAD_PAIR), lambda b, p, i: (b, i, p)),
        out_shape=jax.ShapeDtypeStruct((batch, t_pad, ATTN_WIDTH), BF16),
        scratch_shapes=[
            pltpu.VMEM((2, ROWS, HEAD_PAIR), F32),
            pltpu.VMEM((2, ROWS, 1), F32),
            pltpu.VMEM((2, 2, 2, ROWS, ROWS), F32),
            pltpu.VMEM((2, 2, ROWS, 2 * ROWS), BF16),
            pltpu.SMEM((2,), jnp.int32),
        ],
        compiler_params=pltpu.CompilerParams(
            dimension_semantics=("arbitrary", "arbitrary", "arbitrary"), vmem_limit_bytes=40 * MIB),
        name="attn_prompt",
    )(q, kt, v, u)


def _attn_sample_kernel(pt_ref, bias_ref, q_ref, kn_ref, vn_ref, u_ref, *refs, n_pages):
    del pt_ref
    k_refs = refs[:n_pages]
    v_refs = refs[n_pages:2 * n_pages]
    o_ref = refs[2 * n_pages]
    steps = q_ref.shape[1]
    page = k_refs[0].shape[-1]
    rows = N_HEADS * steps
    contract_lanes = (((1,), (1,)), ((), ()))

    q_all = jnp.concatenate([q_ref[0].astype(F32)] * N_HEADS, axis=0)
    row_head = lax.broadcasted_iota(jnp.int32, (rows, ATTN_WIDTH), 0) // steps
    lane_head = lax.broadcasted_iota(jnp.int32, (rows, ATTN_WIDTH), 1) // HEAD_DIM
    q_bd = jnp.where(row_head == lane_head, q_all, 0.0).astype(BF16)
    key_row = lax.broadcasted_iota(jnp.int32, (rows, page), 0)
    bias = jnp.zeros((rows, page), F32)
    for h in range(N_HEADS):
        bias = jnp.where(key_row // steps == h, bias_ref[h], bias)

    pad = jnp.zeros((page - steps, ATTN_WIDTH), F32)
    k_new = jnp.concatenate([kn_ref[0], pad], axis=0).astype(BF16)
    v_new = jnp.concatenate([vn_ref[0], pad], axis=0).astype(BF16)
    new_mask = lax.broadcasted_iota(jnp.int32, (rows, page), 1) < key_row % steps
    masks = [new_mask] + [None] * n_pages
    zs = [_sb_clamp(lax.dot_general(q_bd, k_new, contract_lanes, preferred_element_type=F32) + bias)]
    for p in reversed(range(n_pages)):
        kt = k_refs[p][...].reshape(ATTN_WIDTH, page).astype(BF16)
        zs.append(_sb_clamp(jnp.dot(q_bd, kt, preferred_element_type=F32) + bias))
    sps = [_sb_softplus(z, mask) for z, mask in zip(zs, masks)]

    later = jnp.dot(jnp.concatenate(sps, axis=0).astype(BF16), u_ref[...], preferred_element_type=F32)
    carry = jnp.zeros((rows, 1), F32)
    ws = []
    for n, (zc, sp, mask) in enumerate(zip(zs, sps, masks)):
        w, carry = _sb_weight(zc, sp, later[n * rows:(n + 1) * rows], carry, mask)
        ws.append(w)

    acc = jnp.dot(ws[0], v_new, preferred_element_type=F32)
    for w, p in zip(ws[1:], reversed(range(n_pages))):
        vt = v_refs[p][...].reshape(ATTN_WIDTH, page).astype(BF16)
        acc = acc + lax.dot_general(w, vt, contract_lanes, preferred_element_type=F32)

    out_lane_head = lax.broadcasted_iota(jnp.int32, (steps, ATTN_WIDTH), 1) // HEAD_DIM
    out = jnp.zeros((steps, ATTN_WIDTH), F32)
    for h in range(N_HEADS):
        out = jnp.where(out_lane_head == h, acc[h * steps:(h + 1) * steps], out)
    o_ref[0] = out


def _attn_sample(page_table, bias2, q, k_new, v_new, u, cache_kt, cache_vt):
    n, steps, _ = q.shape
    n_pages = page_table.shape[1]
    page = cache_kt.shape[-1]
    tok_spec = pl.BlockSpec((1, steps, ATTN_WIDTH), lambda b, pt: (b, 0, 0))

    def page_spec(p):
        return pl.BlockSpec((None, None, N_HEADS, HEAD_DIM, page), lambda b, pt: (0, pt[b, p], 0, 0, 0))

    grid_spec = pltpu.PrefetchScalarGridSpec(
        num_scalar_prefetch=1,
        grid=(n,),
        in_specs=[pl.BlockSpec(memory_space=pltpu.SMEM), tok_spec, tok_spec, tok_spec,
                  pl.BlockSpec((page, page), lambda b, pt: (0, 0))]
        + [page_spec(p) for p in range(n_pages)] * 2,
        out_specs=tok_spec,
    )
    return pl.pallas_call(
        functools.partial(_attn_sample_kernel, n_pages=n_pages),
        grid_spec=grid_spec,
        out_shape=jax.ShapeDtypeStruct((n, steps, ATTN_WIDTH), F32),
        compiler_params=pltpu.CompilerParams(
            dimension_semantics=("arbitrary",), vmem_limit_bytes=48 * MIB),
        name="attn_sample",
    )(page_table, bias2, q, k_new, v_new, u, *([cache_kt] * n_pages), *([cache_vt] * n_pages))


def _outproj_mlp_kernel(attn_ref, lru_ref, x_ref, wo_ref, gpost_ref, gpre_ref, gout_ref, wu_ref, wd_ref,
                        y_ref):
    mixed = (jnp.dot(attn_ref[0].astype(BF16), wo_ref[:ATTN_WIDTH, :], preferred_element_type=F32)
             + jnp.dot(lru_ref[0], wo_ref[ATTN_WIDTH:, :], preferred_element_type=F32))
    x = x_ref[0] + _rmsnorm(mixed, gpost_ref[...])
    hn = _rmsnorm(x, gpre_ref[...]).astype(BF16)
    acc = jnp.zeros_like(x)
    for c in range(D_FF // FF_CHUNK):
        cols = slice(c * FF_CHUNK, (c + 1) * FF_CHUNK)
        up = jnp.maximum(jnp.dot(hn, wu_ref[:, cols], preferred_element_type=F32), 0.0)
        acc = acc + jnp.dot((up * up).astype(BF16), wd_ref[cols, :], preferred_element_type=F32)
    y_ref[0] = x + _rmsnorm(acc, gout_ref[...])


def _outproj_mlp(attn, lru, x, w_out, g_post, g_pre, g_out, w_up, w_down, skip_blocks):
    batch, t, _ = x.shape
    mix_spec = pl.BlockSpec((1, ROWS, ATTN_WIDTH), lambda b, i: (b, i + skip_blocks, 0))
    x_spec = pl.BlockSpec((1, ROWS, D_MODEL), lambda b, i: (b, i, 0))

    def weight_spec(shape):
        return pl.BlockSpec(shape, lambda b, i: (0, 0), pipeline_mode=pl.Buffered(1))

    return pl.pallas_call(
        _outproj_mlp_kernel,
        grid=(batch, t // ROWS),
        in_specs=[
            mix_spec, mix_spec, x_spec,
            weight_spec((D_MODEL, D_MODEL)),
            _const_spec((1, D_MODEL)), _const_spec((1, D_MODEL)), _const_spec((1, D_MODEL)),
            weight_spec((D_MODEL, D_FF)), weight_spec((D_FF, D_MODEL)),
        ],
        out_specs=x_spec,
        out_shape=jax.ShapeDtypeStruct((batch, t, D_MODEL), F32),
        compiler_params=pltpu.CompilerParams(
            dimension_semantics=("arbitrary", "arbitrary"), vmem_limit_bytes=56 * MIB),
        name="outproj_mlp",
    )(attn, lru, x, w_out, g_post, g_pre, g_out, w_up, w_down)


def _block_diag(w):
    n, c, d = w.shape
    eye = jnp.eye(n, dtype=w.dtype)
    return (eye[:, None, :, None] * w[:, :, None, :]).reshape(n * c, n * d)


def _later_keys_matrix(n):
    idx = jnp.arange(n)
    return (idx[:, None] > idx[None, :]).astype(BF16)


def _bias_lanes(bias2):
    pieces = []
    rest = bias2
    for _ in range(BIAS_PIECES):
        piece = rest.astype(BF16).astype(F32)
        pieces.append(piece)
        rest = rest - piece
    k_slot = jnp.zeros((N_HEADS, HEAD_SLOT), F32).at[:, HEAD_DIM:HEAD_DIM + BIAS_PIECES].set(
        jnp.stack(pieces, axis=1))
    q_slot = jnp.zeros((N_HEADS, HEAD_SLOT), F32).at[:, HEAD_DIM:HEAD_DIM + BIAS_PIECES].set(1.0)
    return q_slot.reshape(1, -1), k_slot.reshape(1, -1)


def kernel(x_prompt, x_sample, cache_k, cache_v, state_h, state_conv, page_table, meta_tokens,
           g_mix_pre, g_mix_post, g_mlp_pre, g_mlp_post, w_in, sb_bias, conv_w, conv_b, w_gate_a, b_gate_a,
           w_gate_x, b_gate_x, lru_lambda, w_out, w_up, w_down):
    depth = w_in.shape[0]
    assert depth == 1, "single-layer stack"
    batch, seq, _ = x_prompt.shape
    n_dec, dec_seq, _ = x_sample.shape
    assert seq % ROWS == 0 and (n_dec * dec_seq) % ROWS == 0 and dec_seq >= CONV_WIDTH - 1

    first_row = ROWS - N_META
    n_blk = seq // ROWS + 1
    t_pad = n_blk * ROWS
    meta_blk = jnp.zeros((ROWS, D_MODEL), F32).at[first_row:].set(meta_tokens.astype(F32))

    row = lambda a: a.reshape(1, -1)
    w_in_b = w_in[0].astype(BF16)
    w_out_b = w_out[0].astype(BF16)
    w_up_b = w_up[0].astype(BF16)
    w_down_b = w_down[0].astype(BF16)
    wg = jnp.concatenate([_block_diag(w_gate_a[0]), _block_diag(w_gate_x[0])], axis=1).astype(BF16)
    bg = jnp.concatenate([b_gate_a[0], b_gate_x[0]]).reshape(1, -1)
    lru_args = (conv_w[0], row(conv_b[0]), wg, bg, row(lru_lambda[0]))
    mlp_args = (w_out_b, row(g_mix_post[0]), row(g_mlp_pre[0]), row(g_mlp_post[0]), w_up_b, w_down_b)
    bias2 = sb_bias[0].astype(F32) * LOG2E
    q_extra, k_extra = _bias_lanes(bias2)

    q_p, kt_p, vb_p, k_p, v_p, xl_p, gate_p = _inproj_prompt(
        x_prompt, meta_blk, row(g_mix_pre[0]), w_in_b, q_extra, k_extra, n_blk)
    m_p, h_p = _lru_prompt(xl_p, gate_p, *lru_args, first_row)
    attn_p = _attn_prompt(q_p, kt_p, vb_p, _later_keys_matrix(ROWS), first_row)
    y_prompt = _outproj_mlp(attn_p, m_p, x_prompt, *mlp_args, skip_blocks=1)

    xs = x_sample.reshape(n_dec * dec_seq, D_MODEL)
    q_s, k_s, v_s, xl_s, gate_s = _inproj_sample(xs, row(g_mix_pre[0]), w_in_b)
    by_batch = lambda a: a.reshape(n_dec, dec_seq, -1)
    time_major = lambda a: by_batch(a).swapaxes(0, 1)
    m_s, h_s = _lru_sample(time_major(xl_s), time_major(gate_s), state_conv[0].swapaxes(0, 1), state_h[0],
                           *lru_args)
    feature_major = lambda c: jnp.transpose(c, (0, 1, 3, 4, 2))
    attn_s = _attn_sample(page_table, bias2, by_batch(q_s), by_batch(k_s), by_batch(v_s),
                          _later_keys_matrix(cache_k.shape[2]), feature_major(cache_k), feature_major(cache_v))
    y_sample = _outproj_mlp(attn_s.reshape(1, n_dec * dec_seq, ATTN_WIDTH),
                            m_s.swapaxes(0, 1).reshape(1, n_dec * dec_seq, LRU_WIDTH),
                            xs[None], *mlp_args, skip_blocks=0)[0].reshape(x_sample.shape)

    heads = lambda a, lead: a.reshape(*lead, N_HEADS, HEAD_DIM)
    tail = CONV_WIDTH - 1
    return (
        y_prompt,
        y_sample,
        heads(k_p[:, first_row:], (1, batch, seq + N_META)),
        heads(v_p[:, first_row:], (1, batch, seq + N_META)),
        h_p.reshape(1, batch, LRU_WIDTH),
        xl_p[:, t_pad - tail:][None],
        heads(k_s, (1, n_dec, dec_seq)),
        heads(v_s, (1, n_dec, dec_seq)),
        h_s[None],
        by_batch(xl_s)[:, dec_seq - tail:][None],
    )
```

```python
import functools

import jax
import jax.numpy as jnp
from jax import lax
from jax.experimental import pallas as pl
from jax.experimental.pallas import tpu as pltpu

F32 = jnp.float32
BF16 = jnp.bfloat16

D_MODEL = 1024
N_META = 16
ATTN_WIDTH = 512
LRU_WIDTH = 512
HEAD_DIM = 64
N_HEADS = 8
LRU_BLOCKS = 8
LRU_C = 8.0
CONV_WIDTH = 4
D_FF = 4096
RMS_EPS = 1e-6
PROJ_WIDTH = 3 * ATTN_WIDTH + 2 * LRU_WIDTH

SUBLANES = 8
LANES = 128
ROWS = 256
HEAD_PAIR = 2 * HEAD_DIM
HEAD_SLOT = 128
BIAS_PIECES = 3
LOG2E = 1.4426950408889634
Q_SCALE = HEAD_DIM ** -0.5 * LOG2E
Z2_MAX = 126.0
FF_CHUNK = 1024
QKV_CHUNK = 256
MIB = 1024 * 1024


def _rmsnorm(x, g):
    ms = jnp.mean(x * x, axis=-1, keepdims=True)
    return x * lax.rsqrt(ms + RMS_EPS) * g


def _gelu_tanh(x):
    return x * (0.5 * (1.0 + jnp.tanh(0.7978845608028654 * (x + 0.044715 * (x * x * x)))))


def _sigmoid(x):
    return 1.0 / (1.0 + jnp.exp(-x))


def _softplus(x):
    return jnp.maximum(x, 0.0) + jnp.log(1.0 + jnp.exp(-jnp.abs(x)))


def _inproj(x, g, w):
    hn = _rmsnorm(x, g).astype(BF16)
    return jnp.dot(hn, w, preferred_element_type=F32)


def _pad_heads(x):
    zeros = jnp.zeros((x.shape[0], HEAD_SLOT - HEAD_DIM), x.dtype)
    pieces = []
    for h in range(x.shape[1] // HEAD_DIM):
        pieces += [x[:, h * HEAD_DIM:(h + 1) * HEAD_DIM], zeros]
    return jnp.concatenate(pieces, axis=1)


def _inproj_prompt_kernel(x_ref, meta_ref, g_ref, w_ref, qx_ref, kx_ref, cw_ref, cb_ref, wg_ref, bg_ref, lam_ref,
                          q_ref, kt_ref, vb_ref, k_ref, v_ref, m_ref, hlast_ref, tail_out_ref,
                          h_ref, tail_ref, *, first_row):
    blk = pl.program_id(1)

    @pl.when(blk == 0)
    def _():
        h_ref[...] = jnp.zeros_like(h_ref)
        tail_ref[0] = jnp.zeros(tail_ref.shape[1:], F32)

    x = jnp.where(blk == 0, meta_ref[...], x_ref[0])
    hn = _rmsnorm(x, g_ref[...]).astype(BF16)

    lru_in = jnp.dot(hn, w_ref[:, 3 * ATTN_WIDTH:], preferred_element_type=F32)
    xl = lru_in[:, :LRU_WIDTH]
    xc = _lru_conv(xl, tail_ref[blk % 2], cw_ref, cb_ref[...])
    gates = jnp.dot(xc.astype(BF16), wg_ref[...], preferred_element_type=F32) + bg_ref[...]

    def project(first_col):
        return jnp.dot(hn, w_ref[:, first_col:first_col + QKV_CHUNK], preferred_element_type=F32)

    def q_chunk(c):
        def run():
            slots = slice(c * QKV_CHUNK * HEAD_SLOT // HEAD_DIM, (c + 1) * QKV_CHUNK * HEAD_SLOT // HEAD_DIM)
            q = project(c * QKV_CHUNK)
            q_ref[0, :, slots] = (_pad_heads(q * Q_SCALE) + qx_ref[:, slots]).astype(BF16)
        return run

    def k_chunk(c):
        def run():
            slots = slice(c * QKV_CHUNK * HEAD_SLOT // HEAD_DIM, (c + 1) * QKV_CHUNK * HEAD_SLOT // HEAD_DIM)
            k = project(ATTN_WIDTH + c * QKV_CHUNK)
            k_ref[0, :, c * QKV_CHUNK:(c + 1) * QKV_CHUNK] = k
            kt_ref[0, slots, :] = (_pad_heads(k) + kx_ref[:, slots]).T.astype(BF16)
        return run

    def v_chunk(c):
        def run():
            v = project(2 * ATTN_WIDTH + c * QKV_CHUNK)
            v_ref[0, :, c * QKV_CHUNK:(c + 1) * QKV_CHUNK] = v
            vb_ref[0, :, c * QKV_CHUNK:(c + 1) * QKV_CHUNK] = v.astype(BF16)
        return run

    chunks = [job(c) for job in (q_chunk, k_chunk, v_chunk) for c in range(ATTN_WIDTH // QKV_CHUNK)]
    m, h = _lru_scan(xc, gates, lru_in[:, LRU_WIDTH:], h_ref[...], blk * ROWS - first_row, lam_ref[...],
                     between=chunks)
    tail_ref[(blk + 1) % 2] = xl[ROWS - SUBLANES:]
    tail_out_ref[0] = xl[ROWS - SUBLANES:]
    h_ref[...] = h
    hlast_ref[0] = h
    m_ref[0] = m


def _inproj_sample_kernel(x_ref, g_ref, w_ref, q_ref, k_ref, v_ref, xl_ref, gate_ref):
    proj = _inproj(x_ref[...], g_ref[...], w_ref[...])
    q_ref[...] = (proj[:, :ATTN_WIDTH] * Q_SCALE).astype(BF16)
    k_ref[...] = proj[:, ATTN_WIDTH:2 * ATTN_WIDTH]
    v_ref[...] = proj[:, 2 * ATTN_WIDTH:3 * ATTN_WIDTH]
    xl_ref[...] = proj[:, 3 * ATTN_WIDTH:3 * ATTN_WIDTH + LRU_WIDTH]
    gate_ref[...] = proj[:, 3 * ATTN_WIDTH + LRU_WIDTH:]


def _const_spec(shape):
    return pl.BlockSpec(shape, lambda *_: (0,) * len(shape))


def _inproj_prompt(x_prompt, meta_blk, g, w_in, q_extra, k_extra, conv_w, conv_b, wg, bg, lam, n_blk, first_row):
    batch, _, _ = x_prompt.shape
    t_pad = n_blk * ROWS
    slots = N_HEADS * HEAD_SLOT

    def rows(width, dtype):
        return jax.ShapeDtypeStruct((batch, t_pad, width), dtype)

    row_spec = pl.BlockSpec((1, ROWS, ATTN_WIDTH), lambda b, i: (b, i, 0))
    return pl.pallas_call(
        functools.partial(_inproj_prompt_kernel, first_row=first_row),
        grid=(batch, n_blk),
        in_specs=[
            pl.BlockSpec((1, ROWS, D_MODEL), lambda b, i: (b, jnp.maximum(i - 1, 0), 0)),
            _const_spec((ROWS, D_MODEL)),
            _const_spec((1, D_MODEL)),
            _const_spec((D_MODEL, PROJ_WIDTH)),
            _const_spec((1, slots)),
            _const_spec((1, slots)),
            _const_spec((CONV_WIDTH, LRU_WIDTH)), _const_spec((1, LRU_WIDTH)),
            _const_spec((LRU_WIDTH, 2 * LRU_WIDTH)), _const_spec((1, 2 * LRU_WIDTH)),
            _const_spec((1, LRU_WIDTH)),
        ],
        out_specs=[
            pl.BlockSpec((1, ROWS, slots), lambda b, i: (b, i, 0)),
            pl.BlockSpec((1, slots, ROWS), lambda b, i: (b, 0, i)),
            row_spec, row_spec, row_spec, row_spec,
            pl.BlockSpec((1, 1, LRU_WIDTH), lambda b, i: (b, 0, 0)),
            pl.BlockSpec((1, SUBLANES, LRU_WIDTH), lambda b, i: (b, 0, 0)),
        ],
        out_shape=[
            rows(slots, BF16),
            jax.ShapeDtypeStruct((batch, slots, t_pad), BF16),
            rows(ATTN_WIDTH, BF16), rows(ATTN_WIDTH, F32), rows(ATTN_WIDTH, F32),
            rows(LRU_WIDTH, BF16),
            jax.ShapeDtypeStruct((batch, 1, LRU_WIDTH), F32),
            jax.ShapeDtypeStruct((batch, SUBLANES, LRU_WIDTH), F32),
        ],
        scratch_shapes=[pltpu.VMEM((1, LRU_WIDTH), F32), pltpu.VMEM((2, SUBLANES, LRU_WIDTH), F32)],
        compiler_params=pltpu.CompilerParams(
            dimension_semantics=("arbitrary", "arbitrary"), vmem_limit_bytes=48 * MIB),
        name="inproj_prompt",
    )(x_prompt, meta_blk, g, w_in, q_extra, k_extra, conv_w, conv_b, wg, bg, lam)


def _inproj_sample(x, g, w_in):
    n = x.shape[0]
    row_spec = pl.BlockSpec((ROWS, ATTN_WIDTH), lambda i: (i, 0))
    f32_rows = jax.ShapeDtypeStruct((n, ATTN_WIDTH), F32)
    return pl.pallas_call(
        _inproj_sample_kernel,
        grid=(n // ROWS,),
        in_specs=[
            pl.BlockSpec((ROWS, D_MODEL), lambda i: (i, 0)),
            _const_spec((1, D_MODEL)),
            _const_spec((D_MODEL, PROJ_WIDTH)),
        ],
        out_specs=[row_spec] * 5,
        out_shape=[jax.ShapeDtypeStruct((n, ATTN_WIDTH), BF16), f32_rows, f32_rows, f32_rows, f32_rows],
        compiler_params=pltpu.CompilerParams(
            dimension_semantics=("arbitrary",), vmem_limit_bytes=48 * MIB),
        name="inproj_sample",
    )(x, g, w_in)


def _lru_coeffs(xc, gates, lam):
    r = _sigmoid(gates[:, :LRU_WIDTH])
    i = _sigmoid(gates[:, LRU_WIDTH:])
    log_a = (-LRU_C * _softplus(-lam)) * r
    a = jnp.exp(log_a)
    th = jnp.tanh(log_a)
    b = jnp.sqrt(-2.0 * th / (1.0 - th)) * (i * xc)
    return a, b


def _lru_conv(xl, tail, cw_ref, cb):
    ext = jnp.concatenate([tail, xl], axis=0)
    xc = cb + jnp.zeros_like(xl)
    for j in range(CONV_WIDTH):
        shift = CONV_WIDTH - 1 - j
        shifted = xl if shift == 0 else pltpu.roll(ext, shift, 0)[SUBLANES:]
        xc = xc + shifted * cw_ref[j:j + 1, :]
    return xc


def _lru_scan(xc, gates, gate, h, rows_before, lam, between=()):
    between = list(between)

    def stage_done():
        if between:
            between.pop(0)()

    a, b = _lru_coeffs(xc, gates, lam)
    row = lax.broadcasted_iota(jnp.int32, (ROWS, LRU_WIDTH), 0)
    b = jnp.where(rows_before + row >= 0, b, 0.0)
    stage_done()

    sub = row & (SUBLANES - 1)
    for s in (1, 2, 4):
        keep = sub >= s
        b = jnp.where(keep, a * pltpu.roll(b, s, 0) + b, b)
        a = jnp.where(keep, a * pltpu.roll(a, s, 0), a)
        stage_done()
    hs = []
    for g in range(ROWS // SUBLANES):
        grp = slice(g * SUBLANES, (g + 1) * SUBLANES)
        hg = a[grp] * h + b[grp]
        hs.append(hg)
        h = hg[SUBLANES - 1:]
    stage_done()
    m = (jnp.concatenate(hs, axis=0) * _gelu_tanh(gate)).astype(BF16)
    while between:
        stage_done()
    return m, h


def _lru_sample_kernel(xl_ref, gate_ref, prev_ref, h0_ref, cw_ref, cb_ref, wg_ref, bg_ref, lam_ref,
                       m_ref, hlast_ref):
    steps = xl_ref.shape[0]
    xs = [prev_ref[j] for j in range(CONV_WIDTH - 1)] + [xl_ref[t] for t in range(steps)]
    xcs = []
    for t in range(steps):
        xc = cb_ref[...] + jnp.zeros_like(xs[0])
        for j in range(CONV_WIDTH):
            xc = xc + xs[t + j] * cw_ref[j:j + 1, :]
        xcs.append(xc)
    xc = jnp.concatenate(xcs, axis=0)
    gates = jnp.dot(xc.astype(BF16), wg_ref[...], preferred_element_type=F32) + bg_ref[...]
    a, b = _lru_coeffs(xc, gates, lam_ref[...])
    n = xs[0].shape[0]
    h = h0_ref[...]
    for t in range(steps):
        h = a[t * n:(t + 1) * n] * h + b[t * n:(t + 1) * n]
        m_ref[t] = (h * _gelu_tanh(gate_ref[t])).astype(BF16)
    hlast_ref[...] = h


def _lru_sample(xl_t, gate_t, prev_t, h0, conv_w, conv_b, wg, bg, lam):
    steps, n, _ = xl_t.shape
    return pl.pallas_call(
        _lru_sample_kernel,
        out_shape=[jax.ShapeDtypeStruct((steps, n, LRU_WIDTH), BF16),
                   jax.ShapeDtypeStruct((n, LRU_WIDTH), F32)],
        compiler_params=pltpu.CompilerParams(vmem_limit_bytes=32 * MIB),
        name="lru_sample",
    )(xl_t, gate_t, prev_t, h0, conv_w, conv_b, wg, bg, lam)


def _sb_clamp(z2):
    return lax.clamp(-Z2_MAX, z2, Z2_MAX)


def _sb_softplus(zc, mask):
    sp = jnp.log(1.0 + jnp.exp2(zc))
    return sp if mask is None else jnp.where(mask, sp, 0.0)


def _sb_weight(zc, sp, later, carry, mask):
    total = sp + later + carry
    w = jnp.exp2(zc - LOG2E * total)
    return (w if mask is None else jnp.where(mask, w, 0.0)).astype(BF16), total[:, :1]


def _attn_prompt_kernel(q_ref, kt_ref, v_ref, u_ref, o_ref, acc_ref, carry_ref, z_ref, w_ref, state_ref, *,
                        first_row):
    qi = pl.program_id(2)
    n_blk = pl.num_programs(2)
    heads = range(2)
    u = u_ref[...]
    acc_ref[...] = jnp.zeros_like(acc_ref)
    carry_ref[...] = jnp.zeros_like(carry_ref)
    row = lax.broadcasted_iota(jnp.int32, (ROWS, ROWS), 0)
    col = lax.broadcasted_iota(jnp.int32, (ROWS, ROWS), 1)

    def tile_start(kj):
        return kj * ROWS if isinstance(kj, int) else pl.multiple_of(kj * ROWS, ROWS)

    def logits(e, qj, kj):
        slot = slice(e * HEAD_SLOT, (e + 1) * HEAD_SLOT)
        return _sb_clamp(jnp.dot(q_ref[0, pl.ds(tile_start(qj), ROWS), slot],
                                 kt_ref[0, slot, pl.ds(tile_start(kj), ROWS)], preferred_element_type=F32))

    def step(kjs, masked, last, consume_prev=True):
        masks = [(kj * ROWS + col >= first_row) & (kj * ROWS + col < qi * ROWS + row) if m else None
                 for kj, m in zip(kjs, masked)]
        cur = state_ref[1]
        nxt = 1 - cur
        carries = [carry_ref[e] for e in heads]
        accs = [acc_ref[e] for e in heads]
        zs = [[z_ref[cur, e, s] for s in range(len(kjs))] for e in heads]
        w_prev = [w_ref[cur, e] for e in heads]
        v_prev = v_ref[0, pl.ds(pl.multiple_of(state_ref[0] * ROWS, ROWS), 2 * ROWS), :]
        if last:
            q_ahead = jnp.minimum(qi + 1, n_blk - 1)
            k_ahead = [q_ahead, q_ahead - 1]
        else:
            q_ahead = qi
            k_ahead = [jnp.maximum(kjs[-1] - 1, 0), jnp.maximum(kjs[-1] - 2, 0)]

        pvs = [None for _ in heads]
        z_next = [[None, None] for _ in heads]

        def pv_job(e):
            def run():
                pvs[e] = jnp.dot(w_prev[e], v_prev, preferred_element_type=F32)
            return run

        def z_job(e, s):
            def run():
                z_next[e][s] = logits(e, q_ahead, k_ahead[s])
            return run

        fill = [pv_job(e) for e in heads] if consume_prev else []
        fill += [z_job(e, s) for s in range(2) for e in heads]
        chains = [(t, e) for t in range(len(kjs)) for e in heads]
        sps = [[None] * len(kjs) for _ in heads]
        laters = [[None] * len(kjs) for _ in heads]
        for n, (t, e) in enumerate(chains):
            sps[e][t] = _sb_softplus(zs[e][t], masks[t])
            laters[e][t] = jnp.dot(sps[e][t].astype(BF16), u, preferred_element_type=F32)
            for job in fill[n * len(fill) // len(chains):(n + 1) * len(fill) // len(chains)]:
                job()
        ws = [[] for _ in heads]
        for t, mask in enumerate(masks):
            for e in heads:
                w, carries[e] = _sb_weight(zs[e][t], sps[e][t], laters[e][t], carries[e], mask)
                ws[e].append(w)
        for e in heads:
            carry_ref[e] = carries[e]
            acc = accs[e] + pvs[e] if consume_prev else accs[e]
            if len(kjs) == 2:
                w_ref[nxt, e] = jnp.concatenate(ws[e][::-1], axis=1)
            else:
                acc = acc + jnp.dot(ws[e][0], v_ref[0, pl.ds(tile_start(kjs[0]), ROWS), :],
                                    preferred_element_type=F32)
            acc_ref[e] = acc
            for s in range(2):
                z_ref[nxt, e, s] = z_next[e][s]
        if len(kjs) == 2:
            state_ref[0] = jnp.asarray(kjs[-1], jnp.int32)
        state_ref[1] = nxt

    @pl.when(qi == 0)
    def _():
        state_ref[0] = 0
        state_ref[1] = 0
        for e in heads:
            for s in range(2):
                z_ref[0, e, s] = logits(e, 0, 0)

    odd = qi % 2
    n_plain_pairs = (qi + 1) // 2 - 1 - odd

    @pl.when(qi == 1)
    def _():
        step([1, 0], [True, True], last=True, consume_prev=False)

    @pl.when(qi >= 2)
    def _():
        step([qi, qi - 1], [True, True], last=False, consume_prev=False)

    def plain_pair(t, _):
        kj = qi - 2 * (t + 1)
        step([kj, kj - 1], [False, False], last=False)
        return 0

    lax.fori_loop(0, n_plain_pairs, plain_pair, 0)

    @pl.when(jnp.logical_and(odd == 1, qi >= 3))
    def _():
        step([1, 0], [False, True], last=True)

    @pl.when(qi == 0)
    def _():
        step([0], [True], last=True, consume_prev=False)

    @pl.when(jnp.logical_and(odd == 0, qi >= 2))
    def _():
        step([0], [True], last=True)

    @pl.when(odd == 1)
    def _():
        for e in heads:
            acc_ref[e] += jnp.dot(w_ref[state_ref[1], e], v_ref[0, 0:2 * ROWS, :], preferred_element_type=F32)

    lane = lax.broadcasted_iota(jnp.int32, (ROWS, HEAD_PAIR), 1)
    o_ref[0] = jnp.where(lane < HEAD_DIM, acc_ref[0], acc_ref[1]).astype(BF16)


def _attn_prompt(q, kt, v, u, first_row):
    batch, t_pad, _ = v.shape
    n_blk = t_pad // ROWS
    return pl.pallas_call(
        functools.partial(_attn_prompt_kernel, first_row=first_row),
        grid=(batch, N_HEADS // 2, n_blk),
        in_specs=[
            pl.BlockSpec((1, t_pad, 2 * HEAD_SLOT), lambda b, p, i: (b, 0, p)),
            pl.BlockSpec((1, 2 * HEAD_SLOT, t_pad), lambda b, p, i: (b, p, 0)),
            pl.BlockSpec((1, t_pad, HEAD_PAIR), lambda b, p, i: (b, 0, p)),
            _const_spec((ROWS, ROWS)),
        ],
        out_specs=pl.BlockSpec((1, ROWS, HEAD_PAIR), lambda b, p, i: (b, i, p)),
        out_shape=jax.ShapeDtypeStruct((batch, t_pad, ATTN_WIDTH), BF16),
        scratch_shapes=[
            pltpu.VMEM((2, ROWS, HEAD_PAIR), F32),
            pltpu.VMEM((2, ROWS, 1), F32),
            pltpu.VMEM((2, 2, 2, ROWS, ROWS), F32),
            pltpu.VMEM((2, 2, ROWS, 2 * ROWS), BF16),
            pltpu.SMEM((2,), jnp.int32),
        ],
        compiler_params=pltpu.CompilerParams(
            dimension_semantics=("arbitrary", "arbitrary", "arbitrary"), vmem_limit_bytes=40 * MIB),
        name="attn_prompt",
    )(q, kt, v, u)


def _attn_sample_kernel(pt_ref, bias_ref, q_ref, kn_ref, vn_ref, u_ref, *refs, n_pages):
    del pt_ref
    k_refs = refs[:n_pages]
    v_refs = refs[n_pages:2 * n_pages]
    o_ref = refs[2 * n_pages]
    steps = q_ref.shape[1]
    page = k_refs[0].shape[-1]
    rows = N_HEADS * steps
    contract_lanes = (((1,), (1,)), ((), ()))

    q_all = jnp.concatenate([q_ref[0].astype(F32)] * N_HEADS, axis=0)
    row_head = lax.broadcasted_iota(jnp.int32, (rows, ATTN_WIDTH), 0) // steps
    lane_head = lax.broadcasted_iota(jnp.int32, (rows, ATTN_WIDTH), 1) // HEAD_DIM
    q_bd = jnp.where(row_head == lane_head, q_all, 0.0).astype(BF16)
    key_row = lax.broadcasted_iota(jnp.int32, (rows, page), 0)
    bias = jnp.zeros((rows, page), F32)
    for h in range(N_HEADS):
        bias = jnp.where(key_row // steps == h, bias_ref[h], bias)

    pad = jnp.zeros((page - steps, ATTN_WIDTH), F32)
    k_new = jnp.concatenate([kn_ref[0], pad], axis=0).astype(BF16)
    v_new = jnp.concatenate([vn_ref[0], pad], axis=0).astype(BF16)
    new_mask = lax.broadcasted_iota(jnp.int32, (rows, page), 1) < key_row % steps
    masks = [new_mask] + [None] * n_pages
    zs = [_sb_clamp(lax.dot_general(q_bd, k_new, contract_lanes, preferred_element_type=F32) + bias)]
    for p in reversed(range(n_pages)):
        kt = k_refs[p][...].reshape(ATTN_WIDTH, page).astype(BF16)
        zs.append(_sb_clamp(jnp.dot(q_bd, kt, preferred_element_type=F32) + bias))
    sps = [_sb_softplus(z, mask) for z, mask in zip(zs, masks)]

    later = jnp.dot(jnp.concatenate(sps, axis=0).astype(BF16), u_ref[...], preferred_element_type=F32)
    carry = jnp.zeros((rows, 1), F32)
    ws = []
    for n, (zc, sp, mask) in enumerate(zip(zs, sps, masks)):
        w, carry = _sb_weight(zc, sp, later[n * rows:(n + 1) * rows], carry, mask)
        ws.append(w)

    acc = jnp.dot(ws[0], v_new, preferred_element_type=F32)
    for w, p in zip(ws[1:], reversed(range(n_pages))):
        vt = v_refs[p][...].reshape(ATTN_WIDTH, page).astype(BF16)
        acc = acc + lax.dot_general(w, vt, contract_lanes, preferred_element_type=F32)

    out_lane_head = lax.broadcasted_iota(jnp.int32, (steps, ATTN_WIDTH), 1) // HEAD_DIM
    out = jnp.zeros((steps, ATTN_WIDTH), F32)
    for h in range(N_HEADS):
        out = jnp.where(out_lane_head == h, acc[h * steps:(h + 1) * steps], out)
    o_ref[0] = out


def _attn_sample(page_table, bias2, q, k_new, v_new, u, cache_kt, cache_vt):
    n, steps, _ = q.shape
    n_pages = page_table.shape[1]
    page = cache_kt.shape[-1]
    tok_spec = pl.BlockSpec((1, steps, ATTN_WIDTH), lambda b, pt: (b, 0, 0))

    def page_spec(p):
        return pl.BlockSpec((None, None, N_HEADS, HEAD_DIM, page), lambda b, pt: (0, pt[b, p], 0, 0, 0))

    grid_spec = pltpu.PrefetchScalarGridSpec(
        num_scalar_prefetch=1,
        grid=(n,),
        in_specs=[pl.BlockSpec(memory_space=pltpu.SMEM), tok_spec, tok_spec, tok_spec,
                  pl.BlockSpec((page, page), lambda b, pt: (0, 0))]
        + [page_spec(p) for p in range(n_pages)] * 2,
        out_specs=tok_spec,
    )
    return pl.pallas_call(
        functools.partial(_attn_sample_kernel, n_pages=n_pages),
        grid_spec=grid_spec,
        out_shape=jax.ShapeDtypeStruct((n, steps, ATTN_WIDTH), F32),
        compiler_params=pltpu.CompilerParams(
            dimension_semantics=("arbitrary",), vmem_limit_bytes=48 * MIB),
        name="attn_sample",
    )(page_table, bias2, q, k_new, v_new, u, *([cache_kt] * n_pages), *([cache_vt] * n_pages))


def _outproj_mlp_kernel(attn_ref, lru_ref, x_ref, wo_ref, gpost_ref, gpre_ref, gout_ref, wu_ref, wd_ref,
                        y_ref):
    mixed = (jnp.dot(attn_ref[0].astype(BF16), wo_ref[:ATTN_WIDTH, :], preferred_element_type=F32)
             + jnp.dot(lru_ref[0], wo_ref[ATTN_WIDTH:, :], preferred_element_type=F32))
    x = x_ref[0] + _rmsnorm(mixed, gpost_ref[...])
    hn = _rmsnorm(x, gpre_ref[...]).astype(BF16)
    acc = jnp.zeros_like(x)
    for c in range(D_FF // FF_CHUNK):
        cols = slice(c * FF_CHUNK, (c + 1) * FF_CHUNK)
        up = jnp.maximum(jnp.dot(hn, wu_ref[:, cols], preferred_element_type=F32), 0.0)
        acc = acc + jnp.dot((up * up).astype(BF16), wd_ref[cols, :], preferred_element_type=F32)
    y_ref[0] = x + _rmsnorm(acc, gout_ref[...])


def _outproj_mlp(attn, lru, x, w_out, g_post, g_pre, g_out, w_up, w_down, skip_blocks):
    batch, t, _ = x.shape
    mix_spec = pl.BlockSpec((1, ROWS, ATTN_WIDTH), lambda b, i: (b, i + skip_blocks, 0))
    x_spec = pl.BlockSpec((1, ROWS, D_MODEL), lambda b, i: (b, i, 0))

    def weight_spec(shape):
        return pl.BlockSpec(shape, lambda b, i: (0, 0), pipeline_mode=pl.Buffered(1))

    return pl.pallas_call(
        _outproj_mlp_kernel,
        grid=(batch, t // ROWS),
        in_specs=[
            mix_spec, mix_spec, x_spec,
            weight_spec((D_MODEL, D_MODEL)),
            _const_spec((1, D_MODEL)), _const_spec((1, D_MODEL)), _const_spec((1, D_MODEL)),
            weight_spec((D_MODEL, D_FF)), weight_spec((D_FF, D_MODEL)),
        ],
        out_specs=x_spec,
        out_shape=jax.ShapeDtypeStruct((batch, t, D_MODEL), F32),
        compiler_params=pltpu.CompilerParams(
            dimension_semantics=("arbitrary", "arbitrary"), vmem_limit_bytes=56 * MIB),
        name="outproj_mlp",
    )(attn, lru, x, w_out, g_post, g_pre, g_out, w_up, w_down)


def _block_diag(w):
    n, c, d = w.shape
    eye = jnp.eye(n, dtype=w.dtype)
    return (eye[:, None, :, None] * w[:, :, None, :]).reshape(n * c, n * d)


def _later_keys_matrix(n):
    idx = jnp.arange(n)
    return (idx[:, None] > idx[None, :]).astype(BF16)


def _bias_lanes(bias2):
    pieces = []
    rest = bias2
    for _ in range(BIAS_PIECES):
        piece = rest.astype(BF16).astype(F32)
        pieces.append(piece)
        rest = rest - piece
    k_slot = jnp.zeros((N_HEADS, HEAD_SLOT), F32).at[:, HEAD_DIM:HEAD_DIM + BIAS_PIECES].set(
        jnp.stack(pieces, axis=1))
    q_slot = jnp.zeros((N_HEADS, HEAD_SLOT), F32).at[:, HEAD_DIM:HEAD_DIM + BIAS_PIECES].set(1.0)
    return q_slot.reshape(1, -1), k_slot.reshape(1, -1)


def kernel(x_prompt, x_sample, cache_k, cache_v, state_h, state_conv, page_table, meta_tokens,
           g_mix_pre, g_mix_post, g_mlp_pre, g_mlp_post, w_in, sb_bias, conv_w, conv_b, w_gate_a, b_gate_a,
           w_gate_x, b_gate_x, lru_lambda, w_out, w_up, w_down):
    depth = w_in.shape[0]
    assert depth == 1, "single-layer stack"
    batch, seq, _ = x_prompt.shape
    n_dec, dec_seq, _ = x_sample.shape
    assert seq % ROWS == 0 and (n_dec * dec_seq) % ROWS == 0 and dec_seq >= CONV_WIDTH - 1

    first_row = ROWS - N_META
    n_blk = seq // ROWS + 1
    t_pad = n_blk * ROWS
    meta_blk = jnp.zeros((ROWS, D_MODEL), F32).at[first_row:].set(meta_tokens.astype(F32))

    row = lambda a: a.reshape(1, -1)
    w_in_b = w_in[0].astype(BF16)
    w_out_b = w_out[0].astype(BF16)
    w_up_b = w_up[0].astype(BF16)
    w_down_b = w_down[0].astype(BF16)
    wg = jnp.concatenate([_block_diag(w_gate_a[0]), _block_diag(w_gate_x[0])], axis=1).astype(BF16)
    bg = jnp.concatenate([b_gate_a[0], b_gate_x[0]]).reshape(1, -1)
    lru_args = (conv_w[0], row(conv_b[0]), wg, bg, row(lru_lambda[0]))
    mlp_args = (w_out_b, row(g_mix_post[0]), row(g_mlp_pre[0]), row(g_mlp_post[0]), w_up_b, w_down_b)
    bias2 = sb_bias[0].astype(F32) * LOG2E
    q_extra, k_extra = _bias_lanes(bias2)

    q_p, kt_p, vb_p, k_p, v_p, m_p, h_p, xl_tail_p = _inproj_prompt(
        x_prompt, meta_blk, row(g_mix_pre[0]), w_in_b, q_extra, k_extra, *lru_args, n_blk, first_row)
    attn_p = _attn_prompt(q_p, kt_p, vb_p, _later_keys_matrix(ROWS), first_row)
    y_prompt = _outproj_mlp(attn_p, m_p, x_prompt, *mlp_args, skip_blocks=1)

    xs = x_sample.reshape(n_dec * dec_seq, D_MODEL)
    q_s, k_s, v_s, xl_s, gate_s = _inproj_sample(xs, row(g_mix_pre[0]), w_in_b)
    by_batch = lambda a: a.reshape(n_dec, dec_seq, -1)
    time_major = lambda a: by_batch(a).swapaxes(0, 1)
    m_s, h_s = _lru_sample(time_major(xl_s), time_major(gate_s), state_conv[0].swapaxes(0, 1), state_h[0],
                           *lru_args)
    feature_major = lambda c: jnp.transpose(c, (0, 1, 3, 4, 2))
    attn_s = _attn_sample(page_table, bias2, by_batch(q_s), by_batch(k_s), by_batch(v_s),
                          _later_keys_matrix(cache_k.shape[2]), feature_major(cache_k), feature_major(cache_v))
    y_sample = _outproj_mlp(attn_s.reshape(1, n_dec * dec_seq, ATTN_WIDTH),
                            m_s.swapaxes(0, 1).reshape(1, n_dec * dec_seq, LRU_WIDTH),
                            xs[None], *mlp_args, skip_blocks=0)[0].reshape(x_sample.shape)

    heads = lambda a, lead: a.reshape(*lead, N_HEADS, HEAD_DIM)
    tail = CONV_WIDTH - 1
    return (
        y_prompt,
        y_sample,
        heads(k_p[:, first_row:], (1, batch, seq + N_META)),
        heads(v_p[:, first_row:], (1, batch, seq + N_META)),
        h_p.reshape(1, batch, LRU_WIDTH),
        xl_tail_p[:, SUBLANES - tail:][None],
        heads(k_s, (1, n_dec, dec_seq)),
        heads(v_s, (1, n_dec, dec_seq)),
        h_s[None],
        by_batch(xl_s)[:, dec_seq - tail:][None],
    )
```

```python
import functools

import jax
import jax.numpy as jnp
from jax import lax
from jax.experimental import pallas as pl
from jax.experimental.pallas import tpu as pltpu

F32 = jnp.float32
BF16 = jnp.bfloat16

D_MODEL = 1024
N_META = 16
ATTN_WIDTH = 512
LRU_WIDTH = 512
HEAD_DIM = 64
N_HEADS = 8
LRU_BLOCKS = 8
LRU_C = 8.0
CONV_WIDTH = 4
D_FF = 4096
RMS_EPS = 1e-6
PROJ_WIDTH = 3 * ATTN_WIDTH + 2 * LRU_WIDTH

SUBLANES = 8
LANES = 128
ROWS = 256
HEAD_PAIR = 2 * HEAD_DIM
HEAD_SLOT = 128
BIAS_PIECES = 3
LOG2E = 1.4426950408889634
Q_SCALE = HEAD_DIM ** -0.5 * LOG2E
Z2_MAX = 126.0
FF_CHUNK = 1024
MLP_ROWS = 512
MIB = 1024 * 1024


def _rmsnorm(x, g):
    ms = jnp.mean(x * x, axis=-1, keepdims=True)
    return x * lax.rsqrt(ms + RMS_EPS) * g


def _gelu_tanh(x):
    return x * (0.5 * (1.0 + jnp.tanh(0.7978845608028654 * (x + 0.044715 * (x * x * x)))))


def _sigmoid(x):
    return 1.0 / (1.0 + jnp.exp(-x))


def _softplus(x):
    return jnp.maximum(x, 0.0) + jnp.log(1.0 + jnp.exp(-jnp.abs(x)))


def _inproj(x, g, w):
    hn = _rmsnorm(x, g).astype(BF16)
    return jnp.dot(hn, w, preferred_element_type=F32)


def _pad_heads(x):
    zeros = jnp.zeros((x.shape[0], HEAD_SLOT - HEAD_DIM), x.dtype)
    pieces = []
    for h in range(x.shape[1] // HEAD_DIM):
        pieces += [x[:, h * HEAD_DIM:(h + 1) * HEAD_DIM], zeros]
    return jnp.concatenate(pieces, axis=1)


def _inproj_prompt_kernel(x_ref, meta_ref, g_ref, w_ref, qx_ref, kx_ref, cw_ref, cb_ref, wg_ref, bg_ref, lam_ref,
                          q_ref, kt_ref, vb_ref, k_ref, v_ref, m_ref, hlast_ref, tail_out_ref,
                          h_ref, tail_ref, *, first_row):
    blk = pl.program_id(1)

    @pl.when(blk == 0)
    def _():
        h_ref[...] = jnp.zeros_like(h_ref)
        tail_ref[0] = jnp.zeros(tail_ref.shape[1:], F32)

    x = jnp.where(blk == 0, meta_ref[...], x_ref[0])
    hn = _rmsnorm(x, g_ref[...]).astype(BF16)

    lru_in = jnp.dot(hn, w_ref[:, 3 * ATTN_WIDTH:], preferred_element_type=F32)
    xl = lru_in[:, :LRU_WIDTH]
    xc = _lru_conv(xl, tail_ref[blk % 2], cw_ref, cb_ref[...])
    gates = jnp.dot(xc.astype(BF16), wg_ref[...], preferred_element_type=F32) + bg_ref[...]

    qkv = jnp.dot(hn, w_ref[:, :3 * ATTN_WIDTH], preferred_element_type=F32)
    k = qkv[:, ATTN_WIDTH:2 * ATTN_WIDTH]
    v = qkv[:, 2 * ATTN_WIDTH:]
    q_ref[0] = (_pad_heads(qkv[:, :ATTN_WIDTH] * Q_SCALE) + qx_ref[...]).astype(BF16)
    kt_ref[0] = (_pad_heads(k) + kx_ref[...]).T.astype(BF16)
    vb_ref[0] = v.astype(BF16)
    k_ref[0] = k
    v_ref[0] = v

    m, h = _lru_scan(xc, gates, lru_in[:, LRU_WIDTH:], h_ref[...], blk * ROWS - first_row, lam_ref[...])
    tail_ref[(blk + 1) % 2] = xl[ROWS - SUBLANES:]
    tail_out_ref[0] = xl[ROWS - SUBLANES:]
    h_ref[...] = h
    hlast_ref[0] = h
    m_ref[0] = m


def _inproj_sample_kernel(x_ref, g_ref, w_ref, q_ref, k_ref, v_ref, xl_ref, gate_ref):
    proj = _inproj(x_ref[...], g_ref[...], w_ref[...])
    q_ref[...] = (proj[:, :ATTN_WIDTH] * Q_SCALE).astype(BF16)
    k_ref[...] = proj[:, ATTN_WIDTH:2 * ATTN_WIDTH]
    v_ref[...] = proj[:, 2 * ATTN_WIDTH:3 * ATTN_WIDTH]
    xl_ref[...] = proj[:, 3 * ATTN_WIDTH:3 * ATTN_WIDTH + LRU_WIDTH]
    gate_ref[...] = proj[:, 3 * ATTN_WIDTH + LRU_WIDTH:]


def _const_spec(shape):
    return pl.BlockSpec(shape, lambda *_: (0,) * len(shape))


def _inproj_prompt(x_prompt, meta_blk, g, w_in, q_extra, k_extra, conv_w, conv_b, wg, bg, lam, n_blk, first_row):
    batch, _, _ = x_prompt.shape
    t_pad = n_blk * ROWS
    slots = N_HEADS * HEAD_SLOT

    def rows(width, dtype):
        return jax.ShapeDtypeStruct((batch, t_pad, width), dtype)

    row_spec = pl.BlockSpec((1, ROWS, ATTN_WIDTH), lambda b, i: (b, i, 0))
    return pl.pallas_call(
        functools.partial(_inproj_prompt_kernel, first_row=first_row),
        grid=(batch, n_blk),
        in_specs=[
            pl.BlockSpec((1, ROWS, D_MODEL), lambda b, i: (b, jnp.maximum(i - 1, 0), 0)),
            _const_spec((ROWS, D_MODEL)),
            _const_spec((1, D_MODEL)),
            _const_spec((D_MODEL, PROJ_WIDTH)),
            _const_spec((1, slots)),
            _const_spec((1, slots)),
            _const_spec((CONV_WIDTH, LRU_WIDTH)), _const_spec((1, LRU_WIDTH)),
            _const_spec((LRU_WIDTH, 2 * LRU_WIDTH)), _const_spec((1, 2 * LRU_WIDTH)),
            _const_spec((1, LRU_WIDTH)),
        ],
        out_specs=[
            pl.BlockSpec((1, ROWS, slots), lambda b, i: (b, i, 0)),
            pl.BlockSpec((1, slots, ROWS), lambda b, i: (b, 0, i)),
            row_spec, row_spec, row_spec,
            pl.BlockSpec((1, ROWS, LRU_WIDTH), lambda b, i: (b, jnp.maximum(i - 1, 0), 0)),
            pl.BlockSpec((1, 1, LRU_WIDTH), lambda b, i: (b, 0, 0)),
            pl.BlockSpec((1, SUBLANES, LRU_WIDTH), lambda b, i: (b, 0, 0)),
        ],
        out_shape=[
            rows(slots, BF16),
            jax.ShapeDtypeStruct((batch, slots, t_pad), BF16),
            rows(ATTN_WIDTH, BF16), rows(ATTN_WIDTH, F32), rows(ATTN_WIDTH, F32),
            jax.ShapeDtypeStruct((batch, t_pad - ROWS, LRU_WIDTH), BF16),
            jax.ShapeDtypeStruct((batch, 1, LRU_WIDTH), F32),
            jax.ShapeDtypeStruct((batch, SUBLANES, LRU_WIDTH), F32),
        ],
        scratch_shapes=[pltpu.VMEM((1, LRU_WIDTH), F32), pltpu.VMEM((2, SUBLANES, LRU_WIDTH), F32)],
        compiler_params=pltpu.CompilerParams(
            dimension_semantics=("arbitrary", "arbitrary"), vmem_limit_bytes=48 * MIB),
        name="inproj_prompt",
    )(x_prompt, meta_blk, g, w_in, q_extra, k_extra, conv_w, conv_b, wg, bg, lam)


def _inproj_sample(x, g, w_in):
    n = x.shape[0]
    row_spec = pl.BlockSpec((ROWS, ATTN_WIDTH), lambda i: (i, 0))
    f32_rows = jax.ShapeDtypeStruct((n, ATTN_WIDTH), F32)
    return pl.pallas_call(
        _inproj_sample_kernel,
        grid=(n // ROWS,),
        in_specs=[
            pl.BlockSpec((ROWS, D_MODEL), lambda i: (i, 0)),
            _const_spec((1, D_MODEL)),
            _const_spec((D_MODEL, PROJ_WIDTH)),
        ],
        out_specs=[row_spec] * 5,
        out_shape=[jax.ShapeDtypeStruct((n, ATTN_WIDTH), BF16), f32_rows, f32_rows, f32_rows, f32_rows],
        compiler_params=pltpu.CompilerParams(
            dimension_semantics=("arbitrary",), vmem_limit_bytes=48 * MIB),
        name="inproj_sample",
    )(x, g, w_in)


def _lru_coeffs(xc, gates, lam):
    r = _sigmoid(gates[:, :LRU_WIDTH])
    i = _sigmoid(gates[:, LRU_WIDTH:])
    log_a = (-LRU_C * _softplus(-lam)) * r
    a = jnp.exp(log_a)
    th = jnp.tanh(log_a)
    b = jnp.sqrt(-2.0 * th / (1.0 - th)) * (i * xc)
    return a, b


def _lru_conv(xl, tail, cw_ref, cb):
    ext = jnp.concatenate([tail, xl], axis=0)
    xc = cb + jnp.zeros_like(xl)
    for j in range(CONV_WIDTH):
        shift = CONV_WIDTH - 1 - j
        shifted = xl if shift == 0 else pltpu.roll(ext, shift, 0)[SUBLANES:]
        xc = xc + shifted * cw_ref[j:j + 1, :]
    return xc


def _lru_scan(xc, gates, gate, h, rows_before, lam):
    a, b = _lru_coeffs(xc, gates, lam)
    row = lax.broadcasted_iota(jnp.int32, (ROWS, LRU_WIDTH), 0)
    b = jnp.where(rows_before + row >= 0, b, 0.0)

    sub = row & (SUBLANES - 1)
    for s in (1, 2, 4):
        keep = sub >= s
        b = jnp.where(keep, a * pltpu.roll(b, s, 0) + b, b)
        a = jnp.where(keep, a * pltpu.roll(a, s, 0), a)
    hs = []
    for g in range(ROWS // SUBLANES):
        grp = slice(g * SUBLANES, (g + 1) * SUBLANES)
        hg = a[grp] * h + b[grp]
        hs.append(hg)
        h = hg[SUBLANES - 1:]
    return (jnp.concatenate(hs, axis=0) * _gelu_tanh(gate)).astype(BF16), h


def _lru_sample_kernel(xl_ref, gate_ref, prev_ref, h0_ref, cw_ref, cb_ref, wg_ref, bg_ref, lam_ref,
                       m_ref, hlast_ref):
    steps = xl_ref.shape[0]
    xs = [prev_ref[j] for j in range(CONV_WIDTH - 1)] + [xl_ref[t] for t in range(steps)]
    xcs = []
    for t in range(steps):
        xc = cb_ref[...] + jnp.zeros_like(xs[0])
        for j in range(CONV_WIDTH):
            xc = xc + xs[t + j] * cw_ref[j:j + 1, :]
        xcs.append(xc)
    xc = jnp.concatenate(xcs, axis=0)
    gates = jnp.dot(xc.astype(BF16), wg_ref[...], preferred_element_type=F32) + bg_ref[...]
    a, b = _lru_coeffs(xc, gates, lam_ref[...])
    n = xs[0].shape[0]
    h = h0_ref[...]
    for t in range(steps):
        h = a[t * n:(t + 1) * n] * h + b[t * n:(t + 1) * n]
        m_ref[t] = (h * _gelu_tanh(gate_ref[t])).astype(BF16)
    hlast_ref[...] = h


def _lru_sample(xl_t, gate_t, prev_t, h0, conv_w, conv_b, wg, bg, lam):
    steps, n, _ = xl_t.shape
    return pl.pallas_call(
        _lru_sample_kernel,
        out_shape=[jax.ShapeDtypeStruct((steps, n, LRU_WIDTH), BF16),
                   jax.ShapeDtypeStruct((n, LRU_WIDTH), F32)],
        compiler_params=pltpu.CompilerParams(vmem_limit_bytes=32 * MIB),
        name="lru_sample",
    )(xl_t, gate_t, prev_t, h0, conv_w, conv_b, wg, bg, lam)


def _sb_clamp(z2):
    return lax.clamp(-Z2_MAX, z2, Z2_MAX)


def _sb_softplus(zc, mask):
    sp = jnp.log(1.0 + jnp.exp2(zc))
    return sp if mask is None else jnp.where(mask, sp, 0.0)


def _sb_weight(zc, sp, later, carry, mask):
    total = sp + later + carry
    w = jnp.exp2(zc - LOG2E * total)
    return (w if mask is None else jnp.where(mask, w, 0.0)).astype(BF16), total[:, :1]


def _attn_prompt_kernel(q_ref, kt_ref, v_ref, u_ref, o_ref, acc_ref, carry_ref, z_ref, w_ref, state_ref, *,
                        first_row):
    qi = pl.program_id(2)
    n_blk = pl.num_programs(2)
    heads = range(2)
    u = u_ref[...]
    acc_ref[...] = jnp.zeros_like(acc_ref)
    carry_ref[...] = jnp.zeros_like(carry_ref)
    row = lax.broadcasted_iota(jnp.int32, (ROWS, ROWS), 0)
    col = lax.broadcasted_iota(jnp.int32, (ROWS, ROWS), 1)

    def tile_start(kj):
        return kj * ROWS if isinstance(kj, int) else pl.multiple_of(kj * ROWS, ROWS)

    def logits(e, qj, kj):
        slot = slice(e * HEAD_SLOT, (e + 1) * HEAD_SLOT)
        return _sb_clamp(jnp.dot(q_ref[0, pl.ds(tile_start(qj), ROWS), slot],
                                 kt_ref[0, slot, pl.ds(tile_start(kj), ROWS)], preferred_element_type=F32))

    def step(kjs, masked, last, consume_prev=True):
        cols = [slice(first_row // LANES * LANES if isinstance(kj, int) and kj == 0 else 0, ROWS) for kj in kjs]
        masks = [((kj * ROWS + col >= first_row) & (kj * ROWS + col < qi * ROWS + row))[:, c] if m else None
                 for kj, m, c in zip(kjs, masked, cols)]
        cur = state_ref[1]
        nxt = 1 - cur
        carries = [carry_ref[e] for e in heads]
        accs = [acc_ref[e] for e in heads]
        zs = [[z_ref[cur, e, s, :, cols[s]] for s in range(len(kjs))] for e in heads]
        w_prev = [w_ref[cur, e] for e in heads]
        v_prev = v_ref[0, pl.ds(pl.multiple_of(state_ref[0] * ROWS, ROWS), 2 * ROWS), :]
        if last:
            q_ahead = jnp.minimum(qi + 1, n_blk - 1)
            k_ahead = [q_ahead, q_ahead - 1]
        else:
            q_ahead = qi
            k_ahead = [jnp.maximum(kjs[-1] - 1, 0), jnp.maximum(kjs[-1] - 2, 0)]

        pvs = [None for _ in heads]
        z_next = [[None, None] for _ in heads]

        def pv_job(e):
            def run():
                pvs[e] = jnp.dot(w_prev[e], v_prev, preferred_element_type=F32)
            return run

        def z_job(e, s):
            def run():
                z_next[e][s] = logits(e, q_ahead, k_ahead[s])
            return run

        fill = [pv_job(e) for e in heads] if consume_prev else []
        fill += [z_job(e, s) for s in range(2) for e in heads]
        chains = [(t, e) for t in range(len(kjs)) for e in heads]
        sps = [[None] * len(kjs) for _ in heads]
        laters = [[None] * len(kjs) for _ in heads]
        for n, (t, e) in enumerate(chains):
            sps[e][t] = _sb_softplus(zs[e][t], masks[t])
            width = sps[e][t].shape[1]
            laters[e][t] = jnp.dot(sps[e][t].astype(BF16), u[:width, :width], preferred_element_type=F32)
            for job in fill[n * len(fill) // len(chains):(n + 1) * len(fill) // len(chains)]:
                job()
        ws = [[] for _ in heads]
        for t, mask in enumerate(masks):
            for e in heads:
                w, carries[e] = _sb_weight(zs[e][t], sps[e][t], laters[e][t], carries[e], mask)
                if cols[t].start:
                    w = jnp.concatenate([jnp.zeros((ROWS, cols[t].start), BF16), w], axis=1)
                ws[e].append(w)
        for e in heads:
            carry_ref[e] = carries[e]
            acc = accs[e] + pvs[e] if consume_prev else accs[e]
            if len(kjs) == 2:
                w_ref[nxt, e] = jnp.concatenate(ws[e][::-1], axis=1)
            else:
                acc = acc + jnp.dot(ws[e][0], v_ref[0, pl.ds(tile_start(kjs[0]), ROWS), :],
                                    preferred_element_type=F32)
            acc_ref[e] = acc
            for s in range(2):
                z_ref[nxt, e, s] = z_next[e][s]
        if len(kjs) == 2:
            state_ref[0] = jnp.asarray(kjs[-1], jnp.int32)
        state_ref[1] = nxt

    @pl.when(qi == 0)
    def _():
        state_ref[0] = 0
        state_ref[1] = 0
        for e in heads:
            for s in range(2):
                z_ref[0, e, s] = logits(e, 0, 0)

    odd = qi % 2
    n_plain_pairs = (qi + 1) // 2 - 1 - odd

    @pl.when(qi == 1)
    def _():
        step([1, 0], [True, True], last=True, consume_prev=False)

    @pl.when(qi >= 2)
    def _():
        step([qi, qi - 1], [True, False], last=False, consume_prev=False)

    def plain_pair(t, _):
        kj = qi - 2 * (t + 1)
        step([kj, kj - 1], [False, False], last=False)
        return 0

    lax.fori_loop(0, n_plain_pairs, plain_pair, 0)

    @pl.when(jnp.logical_and(odd == 1, qi >= 3))
    def _():
        step([1, 0], [False, True], last=True)

    @pl.when(qi == 0)
    def _():
        step([0], [True], last=True, consume_prev=False)

    @pl.when(jnp.logical_and(odd == 0, qi >= 2))
    def _():
        step([0], [True], last=True)

    @pl.when(odd == 1)
    def _():
        for e in heads:
            acc_ref[e] += jnp.dot(w_ref[state_ref[1], e], v_ref[0, 0:2 * ROWS, :], preferred_element_type=F32)

    lane = lax.broadcasted_iota(jnp.int32, (ROWS, HEAD_PAIR), 1)
    o_ref[0] = jnp.where(lane < HEAD_DIM, acc_ref[0], acc_ref[1]).astype(BF16)


def _attn_prompt(q, kt, v, u, first_row):
    batch, t_pad, _ = v.shape
    n_blk = t_pad // ROWS
    return pl.pallas_call(
        functools.partial(_attn_prompt_kernel, first_row=first_row),
        grid=(batch, N_HEADS // 2, n_blk),
        in_specs=[
            pl.BlockSpec((1, t_pad, 2 * HEAD_SLOT), lambda b, p, i: (b, 0, p)),
            pl.BlockSpec((1, 2 * HEAD_SLOT, t_pad), lambda b, p, i: (b, p, 0)),
            pl.BlockSpec((1, t_pad, HEAD_PAIR), lambda b, p, i: (b, 0, p)),
            _const_spec((ROWS, ROWS)),
        ],
        out_specs=pl.BlockSpec((1, ROWS, HEAD_PAIR), lambda b, p, i: (b, jnp.maximum(i - 1, 0), p)),
        out_shape=jax.ShapeDtypeStruct((batch, t_pad - ROWS, ATTN_WIDTH), BF16),
        scratch_shapes=[
            pltpu.VMEM((2, ROWS, HEAD_PAIR), F32),
            pltpu.VMEM((2, ROWS, 1), F32),
            pltpu.VMEM((2, 2, 2, ROWS, ROWS), F32),
            pltpu.VMEM((2, 2, ROWS, 2 * ROWS), BF16),
            pltpu.SMEM((2,), jnp.int32),
        ],
        compiler_params=pltpu.CompilerParams(
            dimension_semantics=("arbitrary", "arbitrary", "arbitrary"), vmem_limit_bytes=40 * MIB),
        name="attn_prompt",
    )(q, kt, v, u)


def _attn_sample_kernel(pt_ref, bias_ref, q_ref, kn_ref, vn_ref, u_ref, *refs, n_pages):
    del pt_ref
    k_refs = refs[:n_pages]
    v_refs = refs[n_pages:2 * n_pages]
    o_ref = refs[2 * n_pages]
    steps = q_ref.shape[1]
    page = k_refs[0].shape[-1]
    rows = N_HEADS * steps
    contract_lanes = (((1,), (1,)), ((), ()))

    q_all = jnp.concatenate([q_ref[0].astype(F32)] * N_HEADS, axis=0)
    row_head = lax.broadcasted_iota(jnp.int32, (rows, ATTN_WIDTH), 0) // steps
    lane_head = lax.broadcasted_iota(jnp.int32, (rows, ATTN_WIDTH), 1) // HEAD_DIM
    q_bd = jnp.where(row_head == lane_head, q_all, 0.0).astype(BF16)
    key_row = lax.broadcasted_iota(jnp.int32, (rows, page), 0)
    bias = jnp.zeros((rows, page), F32)
    for h in range(N_HEADS):
        bias = jnp.where(key_row // steps == h, bias_ref[h], bias)

    pad = jnp.zeros((page - steps, ATTN_WIDTH), F32)
    k_new = jnp.concatenate([kn_ref[0], pad], axis=0).astype(BF16)
    v_new = jnp.concatenate([vn_ref[0], pad], axis=0).astype(BF16)
    new_mask = lax.broadcasted_iota(jnp.int32, (rows, page), 1) < key_row % steps
    masks = [new_mask] + [None] * n_pages
    zs = [_sb_clamp(lax.dot_general(q_bd, k_new, contract_lanes, preferred_element_type=F32) + bias)]
    for p in reversed(range(n_pages)):
        kt = k_refs[p][...].reshape(ATTN_WIDTH, page).astype(BF16)
        zs.append(_sb_clamp(jnp.dot(q_bd, kt, preferred_element_type=F32) + bias))
    sps = [_sb_softplus(z, mask) for z, mask in zip(zs, masks)]

    later = jnp.dot(jnp.concatenate(sps, axis=0).astype(BF16), u_ref[...], preferred_element_type=F32)
    carry = jnp.zeros((rows, 1), F32)
    ws = []
    for n, (zc, sp, mask) in enumerate(zip(zs, sps, masks)):
        w, carry = _sb_weight(zc, sp, later[n * rows:(n + 1) * rows], carry, mask)
        ws.append(w)

    acc = jnp.dot(ws[0], v_new, preferred_element_type=F32)
    for w, p in zip(ws[1:], reversed(range(n_pages))):
        vt = v_refs[p][...].reshape(ATTN_WIDTH, page).astype(BF16)
        acc = acc + lax.dot_general(w, vt, contract_lanes, preferred_element_type=F32)

    out_lane_head = lax.broadcasted_iota(jnp.int32, (steps, ATTN_WIDTH), 1) // HEAD_DIM
    out = jnp.zeros((steps, ATTN_WIDTH), F32)
    for h in range(N_HEADS):
        out = jnp.where(out_lane_head == h, acc[h * steps:(h + 1) * steps], out)
    o_ref[0] = out


def _attn_sample(page_table, bias2, q, k_new, v_new, u, cache_kt, cache_vt):
    n, steps, _ = q.shape
    n_pages = page_table.shape[1]
    page = cache_kt.shape[-1]
    tok_spec = pl.BlockSpec((1, steps, ATTN_WIDTH), lambda b, pt: (b, 0, 0))

    def page_spec(p):
        return pl.BlockSpec((None, None, N_HEADS, HEAD_DIM, page), lambda b, pt: (0, pt[b, p], 0, 0, 0))

    grid_spec = pltpu.PrefetchScalarGridSpec(
        num_scalar_prefetch=1,
        grid=(n,),
        in_specs=[pl.BlockSpec(memory_space=pltpu.SMEM), tok_spec, tok_spec, tok_spec,
                  pl.BlockSpec((page, page), lambda b, pt: (0, 0))]
        + [page_spec(p) for p in range(n_pages)] * 2,
        out_specs=tok_spec,
    )
    return pl.pallas_call(
        functools.partial(_attn_sample_kernel, n_pages=n_pages),
        grid_spec=grid_spec,
        out_shape=jax.ShapeDtypeStruct((n, steps, ATTN_WIDTH), F32),
        compiler_params=pltpu.CompilerParams(
            dimension_semantics=("arbitrary",), vmem_limit_bytes=48 * MIB),
        name="attn_sample",
    )(page_table, bias2, q, k_new, v_new, u, *([cache_kt] * n_pages), *([cache_vt] * n_pages))


def _outproj_mlp_kernel(attn_ref, lru_ref, x_ref, wo_ref, gpost_ref, gpre_ref, gout_ref, wu_ref, wd_ref,
                        y_ref):
    mixed = (jnp.dot(attn_ref[0].astype(BF16), wo_ref[:ATTN_WIDTH, :], preferred_element_type=F32)
             + jnp.dot(lru_ref[0], wo_ref[ATTN_WIDTH:, :], preferred_element_type=F32))
    x = x_ref[0] + _rmsnorm(mixed, gpost_ref[...])
    hn = _rmsnorm(x, gpre_ref[...]).astype(BF16)
    acc = jnp.zeros_like(x)
    for c in range(D_FF // FF_CHUNK):
        cols = slice(c * FF_CHUNK, (c + 1) * FF_CHUNK)
        up = jnp.maximum(jnp.dot(hn, wu_ref[:, cols], preferred_element_type=F32), 0.0)
        acc = acc + jnp.dot((up * up).astype(BF16), wd_ref[cols, :], preferred_element_type=F32)
    y_ref[0] = x + _rmsnorm(acc, gout_ref[...])


def _outproj_mlp(attn, lru, x, w_out, g_post, g_pre, g_out, w_up, w_down):
    batch, t, _ = x.shape
    mix_spec = pl.BlockSpec((1, MLP_ROWS, ATTN_WIDTH), lambda b, i: (b, i, 0))
    x_spec = pl.BlockSpec((1, MLP_ROWS, D_MODEL), lambda b, i: (b, i, 0))

    def weight_spec(shape):
        return pl.BlockSpec(shape, lambda b, i: (0, 0), pipeline_mode=pl.Buffered(1))

    return pl.pallas_call(
        _outproj_mlp_kernel,
        grid=(batch, t // MLP_ROWS),
        in_specs=[
            mix_spec, mix_spec, x_spec,
            weight_spec((D_MODEL, D_MODEL)),
            _const_spec((1, D_MODEL)), _const_spec((1, D_MODEL)), _const_spec((1, D_MODEL)),
            weight_spec((D_MODEL, D_FF)), weight_spec((D_FF, D_MODEL)),
        ],
        out_specs=x_spec,
        out_shape=jax.ShapeDtypeStruct((batch, t, D_MODEL), F32),
        compiler_params=pltpu.CompilerParams(
            dimension_semantics=("arbitrary", "arbitrary"), vmem_limit_bytes=56 * MIB),
        name="outproj_mlp",
    )(attn, lru, x, w_out, g_post, g_pre, g_out, w_up, w_down)


def _block_diag(w):
    n, c, d = w.shape
    eye = jnp.eye(n, dtype=w.dtype)
    return (eye[:, None, :, None] * w[:, :, None, :]).reshape(n * c, n * d)


def _later_keys_matrix(n):
    idx = jnp.arange(n)
    return (idx[:, None] > idx[None, :]).astype(BF16)


def _bias_lanes(bias2):
    pieces = []
    rest = bias2
    for _ in range(BIAS_PIECES):
        piece = rest.astype(BF16).astype(F32)
        pieces.append(piece)
        rest = rest - piece
    k_slot = jnp.zeros((N_HEADS, HEAD_SLOT), F32).at[:, HEAD_DIM:HEAD_DIM + BIAS_PIECES].set(
        jnp.stack(pieces, axis=1))
    q_slot = jnp.zeros((N_HEADS, HEAD_SLOT), F32).at[:, HEAD_DIM:HEAD_DIM + BIAS_PIECES].set(1.0)
    return q_slot.reshape(1, -1), k_slot.reshape(1, -1)


def kernel(x_prompt, x_sample, cache_k, cache_v, state_h, state_conv, page_table, meta_tokens,
           g_mix_pre, g_mix_post, g_mlp_pre, g_mlp_post, w_in, sb_bias, conv_w, conv_b, w_gate_a, b_gate_a,
           w_gate_x, b_gate_x, lru_lambda, w_out, w_up, w_down):
    depth = w_in.shape[0]
    assert depth == 1, "single-layer stack"
    batch, seq, _ = x_prompt.shape
    n_dec, dec_seq, _ = x_sample.shape
    assert seq % MLP_ROWS == 0 and (n_dec * dec_seq) % MLP_ROWS == 0 and dec_seq >= CONV_WIDTH - 1

    first_row = ROWS - N_META
    n_blk = seq // ROWS + 1
    t_pad = n_blk * ROWS
    meta_blk = jnp.zeros((ROWS, D_MODEL), F32).at[first_row:].set(meta_tokens.astype(F32))

    row = lambda a: a.reshape(1, -1)
    w_in_b = w_in[0].astype(BF16)
    w_out_b = w_out[0].astype(BF16)
    w_up_b = w_up[0].astype(BF16)
    w_down_b = w_down[0].astype(BF16)
    wg = jnp.concatenate([_block_diag(w_gate_a[0]), _block_diag(w_gate_x[0])], axis=1).astype(BF16)
    bg = jnp.concatenate([b_gate_a[0], b_gate_x[0]]).reshape(1, -1)
    lru_args = (conv_w[0], row(conv_b[0]), wg, bg, row(lru_lambda[0]))
    mlp_args = (w_out_b, row(g_mix_post[0]), row(g_mlp_pre[0]), row(g_mlp_post[0]), w_up_b, w_down_b)
    bias2 = sb_bias[0].astype(F32) * LOG2E
    q_extra, k_extra = _bias_lanes(bias2)

    q_p, kt_p, vb_p, k_p, v_p, m_p, h_p, xl_tail_p = _inproj_prompt(
        x_prompt, meta_blk, row(g_mix_pre[0]), w_in_b, q_extra, k_extra, *lru_args, n_blk, first_row)
    attn_p = _attn_prompt(q_p, kt_p, vb_p, _later_keys_matrix(ROWS), first_row)
    y_prompt = _outproj_mlp(attn_p, m_p, x_prompt, *mlp_args)

    xs = x_sample.reshape(n_dec * dec_seq, D_MODEL)
    q_s, k_s, v_s, xl_s, gate_s = _inproj_sample(xs, row(g_mix_pre[0]), w_in_b)
    by_batch = lambda a: a.reshape(n_dec, dec_seq, -1)
    time_major = lambda a: by_batch(a).swapaxes(0, 1)
    m_s, h_s = _lru_sample(time_major(xl_s), time_major(gate_s), state_conv[0].swapaxes(0, 1), state_h[0],
                           *lru_args)
    feature_major = lambda c: jnp.transpose(c, (0, 1, 3, 4, 2))
    attn_s = _attn_sample(page_table, bias2, by_batch(q_s), by_batch(k_s), by_batch(v_s),
                          _later_keys_matrix(cache_k.shape[2]), feature_major(cache_k), feature_major(cache_v))
    y_sample = _outproj_mlp(attn_s.reshape(1, n_dec * dec_seq, ATTN_WIDTH),
                            m_s.swapaxes(0, 1).reshape(1, n_dec * dec_seq, LRU_WIDTH),
                            xs[None], *mlp_args)[0].reshape(x_sample.shape)

    heads = lambda a, lead: a.reshape(*lead, N_HEADS, HEAD_DIM)
    tail = CONV_WIDTH - 1
    return (
        y_prompt,
        y_sample,
        heads(k_p[:, first_row:], (1, batch, seq + N_META)),
        heads(v_p[:, first_row:], (1, batch, seq + N_META)),
        h_p.reshape(1, batch, LRU_WIDTH),
        xl_tail_p[:, SUBLANES - tail:][None],
        heads(k_s, (1, n_dec, dec_seq)),
        heads(v_s, (1, n_dec, dec_seq)),
        h_s[None],
        by_batch(xl_s)[:, dec_seq - tail:][None],
    )
```

```python
import functools

import jax
import jax.numpy as jnp
from jax import lax
from jax.experimental import pallas as pl
from jax.experimental.pallas import tpu as pltpu

F32 = jnp.float32
BF16 = jnp.bfloat16

D_MODEL = 1024
N_META = 16
ATTN_WIDTH = 512
LRU_WIDTH = 512
HEAD_DIM = 64
N_HEADS = 8
LRU_BLOCKS = 8
LRU_C = 8.0
CONV_WIDTH = 4
D_FF = 4096
RMS_EPS = 1e-6
RSQRT_FLOOR = 1e-30
PROJ_WIDTH = 3 * ATTN_WIDTH + 2 * LRU_WIDTH

SUBLANES = 8
LANES = 128
ROWS = 256
HEAD_PAIR = 2 * HEAD_DIM
HEAD_SLOT = 128
BIAS_PIECES = 3
LOG2E = 1.4426950408889634
Q_SCALE = HEAD_DIM ** -0.5 * LOG2E
Z2_MAX = 126.0
FF_CHUNK = 1024
MLP_ROWS = 512
MIB = 1024 * 1024


def _rmsnorm(x, g):
    ms = jnp.mean(x * x, axis=-1, keepdims=True)
    return x * lax.rsqrt(ms + RMS_EPS) * g


def _gelu_tanh(x):
    return x * (0.5 * (1.0 + jnp.tanh(0.7978845608028654 * (x + 0.044715 * (x * x * x)))))


def _sigmoid(x):
    return 1.0 / (1.0 + jnp.exp(-x))


def _softplus(x):
    return jnp.maximum(x, 0.0) + jnp.log(1.0 + jnp.exp(-jnp.abs(x)))


def _inproj(x, g, w):
    hn = _rmsnorm(x, g).astype(BF16)
    return jnp.dot(hn, w, preferred_element_type=F32)


def _pad_heads(x):
    zeros = jnp.zeros((x.shape[0], HEAD_SLOT - HEAD_DIM), x.dtype)
    pieces = []
    for h in range(x.shape[1] // HEAD_DIM):
        pieces += [x[:, h * HEAD_DIM:(h + 1) * HEAD_DIM], zeros]
    return jnp.concatenate(pieces, axis=1)


def _inproj_prompt_kernel(x_ref, meta_ref, g_ref, w_ref, qx_ref, kx_ref, cw_ref, cb_ref, wg_ref, bg_ref, lam_ref,
                          q_ref, kt_ref, vb_ref, k_ref, v_ref, m_ref, hlast_ref, tail_out_ref,
                          h_ref, tail_ref, *, first_row):
    blk = pl.program_id(1)

    @pl.when(blk == 0)
    def _():
        h_ref[...] = jnp.zeros_like(h_ref)
        tail_ref[0] = jnp.zeros(tail_ref.shape[1:], F32)

    x = jnp.where(blk == 0, meta_ref[...], x_ref[0])
    hn = _rmsnorm(x, g_ref[...]).astype(BF16)

    lru_in = jnp.dot(hn, w_ref[:, 3 * ATTN_WIDTH:], preferred_element_type=F32)
    xl = lru_in[:, :LRU_WIDTH]
    xc = _lru_conv(xl, tail_ref[blk % 2], cw_ref, cb_ref[...])
    gates = jnp.dot(xc.astype(BF16), wg_ref[...], preferred_element_type=F32) + bg_ref[...]

    qkv = jnp.dot(hn, w_ref[:, :3 * ATTN_WIDTH], preferred_element_type=F32)
    k = qkv[:, ATTN_WIDTH:2 * ATTN_WIDTH]
    v = qkv[:, 2 * ATTN_WIDTH:]
    q_ref[0] = (_pad_heads(qkv[:, :ATTN_WIDTH] * Q_SCALE) + qx_ref[...]).astype(BF16)
    kt_ref[0] = (_pad_heads(k) + kx_ref[...]).T.astype(BF16)
    vb_ref[0] = v.astype(BF16)
    k_ref[0] = k
    v_ref[0] = v

    m, h = _lru_scan(xc, gates, lru_in[:, LRU_WIDTH:], h_ref[...], blk * ROWS - first_row, lam_ref[...])
    tail_ref[(blk + 1) % 2] = xl[ROWS - SUBLANES:]
    tail_out_ref[0] = xl[ROWS - SUBLANES:]
    h_ref[...] = h
    hlast_ref[0] = h
    m_ref[0] = m


def _inproj_sample_kernel(x_ref, g_ref, w_ref, q_ref, k_ref, v_ref, xl_ref, gate_ref):
    proj = _inproj(x_ref[...], g_ref[...], w_ref[...])
    q_ref[...] = (proj[:, :ATTN_WIDTH] * Q_SCALE).astype(BF16)
    k_ref[...] = proj[:, ATTN_WIDTH:2 * ATTN_WIDTH]
    v_ref[...] = proj[:, 2 * ATTN_WIDTH:3 * ATTN_WIDTH]
    xl_ref[...] = proj[:, 3 * ATTN_WIDTH:3 * ATTN_WIDTH + LRU_WIDTH]
    gate_ref[...] = proj[:, 3 * ATTN_WIDTH + LRU_WIDTH:]


def _const_spec(shape):
    return pl.BlockSpec(shape, lambda *_: (0,) * len(shape))


def _inproj_prompt(x_prompt, meta_blk, g, w_in, q_extra, k_extra, conv_w, conv_b, wg, bg, lam, n_blk, first_row):
    batch, _, _ = x_prompt.shape
    t_pad = n_blk * ROWS
    slots = N_HEADS * HEAD_SLOT

    def rows(width, dtype):
        return jax.ShapeDtypeStruct((batch, t_pad, width), dtype)

    row_spec = pl.BlockSpec((1, ROWS, ATTN_WIDTH), lambda b, i: (b, i, 0))
    return pl.pallas_call(
        functools.partial(_inproj_prompt_kernel, first_row=first_row),
        grid=(batch, n_blk),
        in_specs=[
            pl.BlockSpec((1, ROWS, D_MODEL), lambda b, i: (b, jnp.maximum(i - 1, 0), 0)),
            _const_spec((ROWS, D_MODEL)),
            _const_spec((1, D_MODEL)),
            _const_spec((D_MODEL, PROJ_WIDTH)),
            _const_spec((1, slots)),
            _const_spec((1, slots)),
            _const_spec((CONV_WIDTH, LRU_WIDTH)), _const_spec((1, LRU_WIDTH)),
            _const_spec((LRU_WIDTH, 2 * LRU_WIDTH)), _const_spec((1, 2 * LRU_WIDTH)),
            _const_spec((1, LRU_WIDTH)),
        ],
        out_specs=[
            pl.BlockSpec((1, ROWS, slots), lambda b, i: (b, i, 0)),
            pl.BlockSpec((1, slots, ROWS), lambda b, i: (b, 0, i)),
            row_spec, row_spec, row_spec,
            pl.BlockSpec((1, ROWS, LRU_WIDTH), lambda b, i: (b, jnp.maximum(i - 1, 0), 0)),
            pl.BlockSpec((1, 1, LRU_WIDTH), lambda b, i: (b, 0, 0)),
            pl.BlockSpec((1, SUBLANES, LRU_WIDTH), lambda b, i: (b, 0, 0)),
        ],
        out_shape=[
            rows(slots, BF16),
            jax.ShapeDtypeStruct((batch, slots, t_pad), BF16),
            rows(ATTN_WIDTH, BF16), rows(ATTN_WIDTH, F32), rows(ATTN_WIDTH, F32),
            jax.ShapeDtypeStruct((batch, t_pad - ROWS, LRU_WIDTH), BF16),
            jax.ShapeDtypeStruct((batch, 1, LRU_WIDTH), F32),
            jax.ShapeDtypeStruct((batch, SUBLANES, LRU_WIDTH), F32),
        ],
        scratch_shapes=[pltpu.VMEM((1, LRU_WIDTH), F32), pltpu.VMEM((2, SUBLANES, LRU_WIDTH), F32)],
        compiler_params=pltpu.CompilerParams(
            dimension_semantics=("arbitrary", "arbitrary"), vmem_limit_bytes=48 * MIB),
        name="inproj_prompt",
    )(x_prompt, meta_blk, g, w_in, q_extra, k_extra, conv_w, conv_b, wg, bg, lam)


def _inproj_sample(x, g, w_in):
    n = x.shape[0]
    row_spec = pl.BlockSpec((ROWS, ATTN_WIDTH), lambda i: (i, 0))
    f32_rows = jax.ShapeDtypeStruct((n, ATTN_WIDTH), F32)
    return pl.pallas_call(
        _inproj_sample_kernel,
        grid=(n // ROWS,),
        in_specs=[
            pl.BlockSpec((ROWS, D_MODEL), lambda i: (i, 0)),
            _const_spec((1, D_MODEL)),
            _const_spec((D_MODEL, PROJ_WIDTH)),
        ],
        out_specs=[row_spec] * 5,
        out_shape=[jax.ShapeDtypeStruct((n, ATTN_WIDTH), BF16), f32_rows, f32_rows, f32_rows, f32_rows],
        compiler_params=pltpu.CompilerParams(
            dimension_semantics=("arbitrary",), vmem_limit_bytes=48 * MIB),
        name="inproj_sample",
    )(x, g, w_in)


def _lru_coeffs(xc, gates, lam):
    r = _sigmoid(gates[:, :LRU_WIDTH])
    i = _sigmoid(gates[:, LRU_WIDTH:])
    log_a = (-LRU_C * _softplus(-lam)) * r
    a = jnp.exp(log_a)
    th = jnp.tanh(log_a)
    num = -2.0 * th
    b = num * lax.rsqrt(jnp.maximum(num * (1.0 - th), RSQRT_FLOOR)) * (i * xc)
    return a, b


def _lru_conv(xl, tail, cw_ref, cb):
    ext = jnp.concatenate([tail, xl], axis=0)
    xc = cb + jnp.zeros_like(xl)
    for j in range(CONV_WIDTH):
        shift = CONV_WIDTH - 1 - j
        shifted = xl if shift == 0 else pltpu.roll(ext, shift, 0)[SUBLANES:]
        xc = xc + shifted * cw_ref[j:j + 1, :]
    return xc


def _lru_scan(xc, gates, gate, h, rows_before, lam):
    a, b = _lru_coeffs(xc, gates, lam)
    row = lax.broadcasted_iota(jnp.int32, (ROWS, LRU_WIDTH), 0)
    b = jnp.where(rows_before + row >= 0, b, 0.0)

    groups = ROWS // SUBLANES
    a = a.reshape(groups, SUBLANES, LRU_WIDTH)
    b = b.reshape(groups, SUBLANES, LRU_WIDTH)
    sub = lax.broadcasted_iota(jnp.int32, a.shape, 1)
    for s in (1, 2, 4):
        keep = sub >= s
        b = jnp.where(keep, a * pltpu.roll(b, s, 1) + b, b)
        a = jnp.where(keep, a * pltpu.roll(a, s, 1), a)
    hs = []
    for g in range(groups):
        hg = a[g] * h + b[g]
        hs.append(hg)
        h = hg[SUBLANES - 1:]
    return (jnp.concatenate(hs, axis=0) * _gelu_tanh(gate)).astype(BF16), h


def _lru_sample_kernel(xl_ref, gate_ref, prev_ref, h0_ref, cw_ref, cb_ref, wg_ref, bg_ref, lam_ref,
                       m_ref, hlast_ref):
    steps = xl_ref.shape[0]
    xs = [prev_ref[j] for j in range(CONV_WIDTH - 1)] + [xl_ref[t] for t in range(steps)]
    xcs = []
    for t in range(steps):
        xc = cb_ref[...] + jnp.zeros_like(xs[0])
        for j in range(CONV_WIDTH):
            xc = xc + xs[t + j] * cw_ref[j:j + 1, :]
        xcs.append(xc)
    xc = jnp.concatenate(xcs, axis=0)
    gates = jnp.dot(xc.astype(BF16), wg_ref[...], preferred_element_type=F32) + bg_ref[...]
    a, b = _lru_coeffs(xc, gates, lam_ref[...])
    n = xs[0].shape[0]
    h = h0_ref[...]
    for t in range(steps):
        h = a[t * n:(t + 1) * n] * h + b[t * n:(t + 1) * n]
        m_ref[t] = (h * _gelu_tanh(gate_ref[t])).astype(BF16)
    hlast_ref[...] = h


def _lru_sample(xl_t, gate_t, prev_t, h0, conv_w, conv_b, wg, bg, lam):
    steps, n, _ = xl_t.shape
    return pl.pallas_call(
        _lru_sample_kernel,
        out_shape=[jax.ShapeDtypeStruct((steps, n, LRU_WIDTH), BF16),
                   jax.ShapeDtypeStruct((n, LRU_WIDTH), F32)],
        compiler_params=pltpu.CompilerParams(vmem_limit_bytes=32 * MIB),
        name="lru_sample",
    )(xl_t, gate_t, prev_t, h0, conv_w, conv_b, wg, bg, lam)


def _sb_clamp(z2):
    return lax.clamp(-Z2_MAX, z2, Z2_MAX)


def _sb_softplus(zc, mask):
    sp = jnp.log(1.0 + jnp.exp2(zc))
    return sp if mask is None else jnp.where(mask, sp, 0.0)


def _sb_weight(zc, sp, later, carry, mask):
    total = sp + later + carry
    w = jnp.exp2(zc - LOG2E * total)
    return (w if mask is None else jnp.where(mask, w, 0.0)).astype(BF16), total[:, :1]


def _attn_prompt_kernel(q_ref, kt_ref, v_ref, u_ref, o_ref, acc_ref, carry_ref, z_ref, w_ref, state_ref, *,
                        first_row):
    qi = pl.program_id(2)
    n_blk = pl.num_programs(2)
    heads = range(2)
    u = u_ref[...]
    acc_ref[...] = jnp.zeros_like(acc_ref)
    carry_ref[...] = jnp.zeros_like(carry_ref)
    row = lax.broadcasted_iota(jnp.int32, (ROWS, ROWS), 0)
    col = lax.broadcasted_iota(jnp.int32, (ROWS, ROWS), 1)

    def tile_start(kj):
        return kj * ROWS if isinstance(kj, int) else pl.multiple_of(kj * ROWS, ROWS)

    def logits(e, qj, kj):
        slot = slice(e * HEAD_SLOT, (e + 1) * HEAD_SLOT)
        return _sb_clamp(jnp.dot(q_ref[0, pl.ds(tile_start(qj), ROWS), slot],
                                 kt_ref[0, slot, pl.ds(tile_start(kj), ROWS)], preferred_element_type=F32))

    def step(kjs, masked, last, consume_prev=True):
        cols = [slice(first_row // LANES * LANES if isinstance(kj, int) and kj == 0 else 0, ROWS) for kj in kjs]
        masks = [((kj * ROWS + col >= first_row) & (kj * ROWS + col < qi * ROWS + row))[:, c] if m else None
                 for kj, m, c in zip(kjs, masked, cols)]
        cur = state_ref[1]
        nxt = 1 - cur
        carries = [carry_ref[e] for e in heads]
        accs = [acc_ref[e] for e in heads]
        zs = [[z_ref[cur, e, s, :, cols[s]] for s in range(len(kjs))] for e in heads]
        w_prev = [w_ref[cur, e] for e in heads]
        v_prev = v_ref[0, pl.ds(pl.multiple_of(state_ref[0] * ROWS, ROWS), 2 * ROWS), :]
        if last:
            q_ahead = jnp.minimum(qi + 1, n_blk - 1)
            k_ahead = [q_ahead, q_ahead - 1]
        else:
            q_ahead = qi
            k_ahead = [jnp.maximum(kjs[-1] - 1, 0), jnp.maximum(kjs[-1] - 2, 0)]

        pvs = [None for _ in heads]
        z_next = [[None, None] for _ in heads]

        def pv_job(e):
            def run():
                pvs[e] = jnp.dot(w_prev[e], v_prev, preferred_element_type=F32)
            return run

        def z_job(e, s):
            def run():
                z_next[e][s] = logits(e, q_ahead, k_ahead[s])
            return run

        fill = [pv_job(e) for e in heads] if consume_prev else []
        fill += [z_job(e, s) for s in range(2) for e in heads]
        chains = [(t, e) for t in range(len(kjs)) for e in heads]
        sps = [[None] * len(kjs) for _ in heads]
        laters = [[None] * len(kjs) for _ in heads]
        for n, (t, e) in enumerate(chains):
            sps[e][t] = _sb_softplus(zs[e][t], masks[t])
            width = sps[e][t].shape[1]
            laters[e][t] = jnp.dot(sps[e][t].astype(BF16), u[:width, :width], preferred_element_type=F32)
            for job in fill[n * len(fill) // len(chains):(n + 1) * len(fill) // len(chains)]:
                job()
        ws = [[] for _ in heads]
        for t, mask in enumerate(masks):
            for e in heads:
                w, carries[e] = _sb_weight(zs[e][t], sps[e][t], laters[e][t], carries[e], mask)
                if cols[t].start:
                    w = jnp.concatenate([jnp.zeros((ROWS, cols[t].start), BF16), w], axis=1)
                ws[e].append(w)
        for e in heads:
            carry_ref[e] = carries[e]
            acc = accs[e] + pvs[e] if consume_prev else accs[e]
            if len(kjs) == 2:
                w_ref[nxt, e] = jnp.concatenate(ws[e][::-1], axis=1)
            else:
                acc = acc + jnp.dot(ws[e][0], v_ref[0, pl.ds(tile_start(kjs[0]), ROWS), :],
                                    preferred_element_type=F32)
            acc_ref[e] = acc
            for s in range(2):
                z_ref[nxt, e, s] = z_next[e][s]
        if len(kjs) == 2:
            state_ref[0] = jnp.asarray(kjs[-1], jnp.int32)
        state_ref[1] = nxt

    @pl.when(qi == 0)
    def _():
        state_ref[0] = 0
        state_ref[1] = 0
        for e in heads:
            for s in range(2):
                z_ref[0, e, s] = logits(e, 0, 0)

    odd = qi % 2
    n_plain_pairs = (qi + 1) // 2 - 1 - odd

    @pl.when(qi == 1)
    def _():
        step([1, 0], [True, True], last=True, consume_prev=False)

    @pl.when(qi >= 2)
    def _():
        step([qi, qi - 1], [True, False], last=False, consume_prev=False)

    def plain_pair(t, _):
        kj = qi - 2 * (t + 1)
        step([kj, kj - 1], [False, False], last=False)
        return 0

    lax.fori_loop(0, n_plain_pairs, plain_pair, 0)

    @pl.when(jnp.logical_and(odd == 1, qi >= 3))
    def _():
        step([1, 0], [False, True], last=True)

    @pl.when(qi == 0)
    def _():
        step([0], [True], last=True, consume_prev=False)

    @pl.when(jnp.logical_and(odd == 0, qi >= 2))
    def _():
        step([0], [True], last=True)

    @pl.when(odd == 1)
    def _():
        for e in heads:
            acc_ref[e] += jnp.dot(w_ref[state_ref[1], e], v_ref[0, 0:2 * ROWS, :], preferred_element_type=F32)

    lane = lax.broadcasted_iota(jnp.int32, (ROWS, HEAD_PAIR), 1)
    o_ref[0] = jnp.where(lane < HEAD_DIM, acc_ref[0], acc_ref[1]).astype(BF16)


def _attn_prompt(q, kt, v, u, first_row):
    batch, t_pad, _ = v.shape
    n_blk = t_pad // ROWS
    return pl.pallas_call(
        functools.partial(_attn_prompt_kernel, first_row=first_row),
        grid=(batch, N_HEADS // 2, n_blk),
        in_specs=[
            pl.BlockSpec((1, t_pad, 2 * HEAD_SLOT), lambda b, p, i: (b, 0, p)),
            pl.BlockSpec((1, 2 * HEAD_SLOT, t_pad), lambda b, p, i: (b, p, 0)),
            pl.BlockSpec((1, t_pad, HEAD_PAIR), lambda b, p, i: (b, 0, p)),
            _const_spec((ROWS, ROWS)),
        ],
        out_specs=pl.BlockSpec((1, ROWS, HEAD_PAIR), lambda b, p, i: (b, jnp.maximum(i - 1, 0), p)),
        out_shape=jax.ShapeDtypeStruct((batch, t_pad - ROWS, ATTN_WIDTH), BF16),
        scratch_shapes=[
            pltpu.VMEM((2, ROWS, HEAD_PAIR), F32),
            pltpu.VMEM((2, ROWS, 1), F32),
            pltpu.VMEM((2, 2, 2, ROWS, ROWS), F32),
            pltpu.VMEM((2, 2, ROWS, 2 * ROWS), BF16),
            pltpu.SMEM((2,), jnp.int32),
        ],
        compiler_params=pltpu.CompilerParams(
            dimension_semantics=("arbitrary", "arbitrary", "arbitrary"), vmem_limit_bytes=40 * MIB),
        name="attn_prompt",
    )(q, kt, v, u)


def _attn_sample_kernel(pt_ref, bias_ref, q_ref, kn_ref, vn_ref, u_ref, *refs, n_pages):
    del pt_ref
    k_refs = refs[:n_pages]
    v_refs = refs[n_pages:2 * n_pages]
    o_ref = refs[2 * n_pages]
    steps = q_ref.shape[1]
    page = k_refs[0].shape[-1]
    rows = N_HEADS * steps
    contract_lanes = (((1,), (1,)), ((), ()))

    q_all = jnp.concatenate([q_ref[0].astype(F32)] * N_HEADS, axis=0)
    row_head = lax.broadcasted_iota(jnp.int32, (rows, ATTN_WIDTH), 0) // steps
    lane_head = lax.broadcasted_iota(jnp.int32, (rows, ATTN_WIDTH), 1) // HEAD_DIM
    q_bd = jnp.where(row_head == lane_head, q_all, 0.0).astype(BF16)
    key_row = lax.broadcasted_iota(jnp.int32, (rows, page), 0)
    bias = jnp.zeros((rows, page), F32)
    for h in range(N_HEADS):
        bias = jnp.where(key_row // steps == h, bias_ref[h], bias)

    pad = jnp.zeros((page - steps, ATTN_WIDTH), F32)
    k_new = jnp.concatenate([kn_ref[0], pad], axis=0).astype(BF16)
    v_new = jnp.concatenate([vn_ref[0], pad], axis=0).astype(BF16)
    new_mask = lax.broadcasted_iota(jnp.int32, (rows, page), 1) < key_row % steps
    masks = [new_mask] + [None] * n_pages
    zs = [_sb_clamp(lax.dot_general(q_bd, k_new, contract_lanes, preferred_element_type=F32) + bias)]
    for p in reversed(range(n_pages)):
        kt = k_refs[p][...].reshape(ATTN_WIDTH, page).astype(BF16)
        zs.append(_sb_clamp(jnp.dot(q_bd, kt, preferred_element_type=F32) + bias))
    sps = [_sb_softplus(z, mask) for z, mask in zip(zs, masks)]

    later = jnp.dot(jnp.concatenate(sps, axis=0).astype(BF16), u_ref[...], preferred_element_type=F32)
    carry = jnp.zeros((rows, 1), F32)
    ws = []
    for n, (zc, sp, mask) in enumerate(zip(zs, sps, masks)):
        w, carry = _sb_weight(zc, sp, later[n * rows:(n + 1) * rows], carry, mask)
        ws.append(w)

    acc = jnp.dot(ws[0], v_new, preferred_element_type=F32)
    for w, p in zip(ws[1:], reversed(range(n_pages))):
        vt = v_refs[p][...].reshape(ATTN_WIDTH, page).astype(BF16)
        acc = acc + lax.dot_general(w, vt, contract_lanes, preferred_element_type=F32)

    out_lane_head = lax.broadcasted_iota(jnp.int32, (steps, ATTN_WIDTH), 1) // HEAD_DIM
    out = jnp.zeros((steps, ATTN_WIDTH), F32)
    for h in range(N_HEADS):
        out = jnp.where(out_lane_head == h, acc[h * steps:(h + 1) * steps], out)
    o_ref[0] = out


def _attn_sample(page_table, bias2, q, k_new, v_new, u, cache_kt, cache_vt):
    n, steps, _ = q.shape
    n_pages = page_table.shape[1]
    page = cache_kt.shape[-1]
    tok_spec = pl.BlockSpec((1, steps, ATTN_WIDTH), lambda b, pt: (b, 0, 0))

    def page_spec(p):
        return pl.BlockSpec((None, None, N_HEADS, HEAD_DIM, page), lambda b, pt: (0, pt[b, p], 0, 0, 0))

    grid_spec = pltpu.PrefetchScalarGridSpec(
        num_scalar_prefetch=1,
        grid=(n,),
        in_specs=[pl.BlockSpec(memory_space=pltpu.SMEM), tok_spec, tok_spec, tok_spec,
                  pl.BlockSpec((page, page), lambda b, pt: (0, 0))]
        + [page_spec(p) for p in range(n_pages)] * 2,
        out_specs=tok_spec,
    )
    return pl.pallas_call(
        functools.partial(_attn_sample_kernel, n_pages=n_pages),
        grid_spec=grid_spec,
        out_shape=jax.ShapeDtypeStruct((n, steps, ATTN_WIDTH), F32),
        compiler_params=pltpu.CompilerParams(
            dimension_semantics=("arbitrary",), vmem_limit_bytes=48 * MIB),
        name="attn_sample",
    )(page_table, bias2, q, k_new, v_new, u, *([cache_kt] * n_pages), *([cache_vt] * n_pages))


def _outproj_mlp_kernel(attn_ref, lru_ref, x_ref, wo_ref, gpost_ref, gpre_ref, gout_ref, wu_ref, wd_ref,
                        y_ref):
    mixed = (jnp.dot(attn_ref[0].astype(BF16), wo_ref[:ATTN_WIDTH, :], preferred_element_type=F32)
             + jnp.dot(lru_ref[0], wo_ref[ATTN_WIDTH:, :], preferred_element_type=F32))
    x = x_ref[0] + _rmsnorm(mixed, gpost_ref[...])
    hn = _rmsnorm(x, gpre_ref[...]).astype(BF16)
    acc = jnp.zeros_like(x)
    for c in range(D_FF // FF_CHUNK):
        cols = slice(c * FF_CHUNK, (c + 1) * FF_CHUNK)
        up = jnp.maximum(jnp.dot(hn, wu_ref[:, cols], preferred_element_type=F32), 0.0)
        acc = acc + jnp.dot((up * up).astype(BF16), wd_ref[cols, :], preferred_element_type=F32)
    y_ref[0] = x + _rmsnorm(acc, gout_ref[...])


def _outproj_mlp(attn, lru, x, w_out, g_post, g_pre, g_out, w_up, w_down):
    batch, t, _ = x.shape
    mix_spec = pl.BlockSpec((1, MLP_ROWS, ATTN_WIDTH), lambda b, i: (b, i, 0))
    x_spec = pl.BlockSpec((1, MLP_ROWS, D_MODEL), lambda b, i: (b, i, 0))

    def weight_spec(shape):
        return pl.BlockSpec(shape, lambda b, i: (0, 0), pipeline_mode=pl.Buffered(1))

    return pl.pallas_call(
        _outproj_mlp_kernel,
        grid=(batch, t // MLP_ROWS),
        in_specs=[
            mix_spec, mix_spec, x_spec,
            weight_spec((D_MODEL, D_MODEL)),
            _const_spec((1, D_MODEL)), _const_spec((1, D_MODEL)), _const_spec((1, D_MODEL)),
            weight_spec((D_MODEL, D_FF)), weight_spec((D_FF, D_MODEL)),
        ],
        out_specs=x_spec,
        out_shape=jax.ShapeDtypeStruct((batch, t, D_MODEL), F32),
        compiler_params=pltpu.CompilerParams(
            dimension_semantics=("arbitrary", "arbitrary"), vmem_limit_bytes=56 * MIB),
        name="outproj_mlp",
    )(attn, lru, x, w_out, g_post, g_pre, g_out, w_up, w_down)


def _block_diag(w):
    n, c, d = w.shape
    eye = jnp.eye(n, dtype=w.dtype)
    return (eye[:, None, :, None] * w[:, :, None, :]).reshape(n * c, n * d)


def _later_keys_matrix(n):
    idx = jnp.arange(n)
    return (idx[:, None] > idx[None, :]).astype(BF16)


def _bias_lanes(bias2):
    pieces = []
    rest = bias2
    for _ in range(BIAS_PIECES):
        piece = rest.astype(BF16).astype(F32)
        pieces.append(piece)
        rest = rest - piece
    k_slot = jnp.zeros((N_HEADS, HEAD_SLOT), F32).at[:, HEAD_DIM:HEAD_DIM + BIAS_PIECES].set(
        jnp.stack(pieces, axis=1))
    q_slot = jnp.zeros((N_HEADS, HEAD_SLOT), F32).at[:, HEAD_DIM:HEAD_DIM + BIAS_PIECES].set(1.0)
    return q_slot.reshape(1, -1), k_slot.reshape(1, -1)


def kernel(x_prompt, x_sample, cache_k, cache_v, state_h, state_conv, page_table, meta_tokens,
           g_mix_pre, g_mix_post, g_mlp_pre, g_mlp_post, w_in, sb_bias, conv_w, conv_b, w_gate_a, b_gate_a,
           w_gate_x, b_gate_x, lru_lambda, w_out, w_up, w_down):
    depth = w_in.shape[0]
    assert depth == 1, "single-layer stack"
    batch, seq, _ = x_prompt.shape
    n_dec, dec_seq, _ = x_sample.shape
    assert seq % MLP_ROWS == 0 and (n_dec * dec_seq) % MLP_ROWS == 0 and dec_seq >= CONV_WIDTH - 1

    first_row = ROWS - N_META
    n_blk = seq // ROWS + 1
    t_pad = n_blk * ROWS
    meta_blk = jnp.zeros((ROWS, D_MODEL), F32).at[first_row:].set(meta_tokens.astype(F32))

    row = lambda a: a.reshape(1, -1)
    w_in_b = w_in[0].astype(BF16)
    w_out_b = w_out[0].astype(BF16)
    w_up_b = w_up[0].astype(BF16)
    w_down_b = w_down[0].astype(BF16)
    wg = jnp.concatenate([_block_diag(w_gate_a[0]), _block_diag(w_gate_x[0])], axis=1).astype(BF16)
    bg = jnp.concatenate([b_gate_a[0], b_gate_x[0]]).reshape(1, -1)
    lru_args = (conv_w[0], row(conv_b[0]), wg, bg, row(lru_lambda[0]))
    mlp_args = (w_out_b, row(g_mix_post[0]), row(g_mlp_pre[0]), row(g_mlp_post[0]), w_up_b, w_down_b)
    bias2 = sb_bias[0].astype(F32) * LOG2E
    q_extra, k_extra = _bias_lanes(bias2)

    q_p, kt_p, vb_p, k_p, v_p, m_p, h_p, xl_tail_p = _inproj_prompt(
        x_prompt, meta_blk, row(g_mix_pre[0]), w_in_b, q_extra, k_extra, *lru_args, n_blk, first_row)
    attn_p = _attn_prompt(q_p, kt_p, vb_p, _later_keys_matrix(ROWS), first_row)
    y_prompt = _outproj_mlp(attn_p, m_p, x_prompt, *mlp_args)

    xs = x_sample.reshape(n_dec * dec_seq, D_MODEL)
    q_s, k_s, v_s, xl_s, gate_s = _inproj_sample(xs, row(g_mix_pre[0]), w_in_b)
    by_batch = lambda a: a.reshape(n_dec, dec_seq, -1)
    time_major = lambda a: by_batch(a).swapaxes(0, 1)
    m_s, h_s = _lru_sample(time_major(xl_s), time_major(gate_s), state_conv[0].swapaxes(0, 1), state_h[0],
                           *lru_args)
    feature_major = lambda c: jnp.transpose(c, (0, 1, 3, 4, 2))
    attn_s = _attn_sample(page_table, bias2, by_batch(q_s), by_batch(k_s), by_batch(v_s),
                          _later_keys_matrix(cache_k.shape[2]), feature_major(cache_k), feature_major(cache_v))
    y_sample = _outproj_mlp(attn_s.reshape(1, n_dec * dec_seq, ATTN_WIDTH),
                            m_s.swapaxes(0, 1).reshape(1, n_dec * dec_seq, LRU_WIDTH),
                            xs[None], *mlp_args)[0].reshape(x_sample.shape)

    heads = lambda a, lead: a.reshape(*lead, N_HEADS, HEAD_DIM)
    tail = CONV_WIDTH - 1
    return (
        y_prompt,
        y_sample,
        heads(k_p[:, first_row:], (1, batch, seq + N_META)),
        heads(v_p[:, first_row:], (1, batch, seq + N_META)),
        h_p.reshape(1, batch, LRU_WIDTH),
        xl_tail_p[:, SUBLANES - tail:][None],
        heads(k_s, (1, n_dec, dec_seq)),
        heads(v_s, (1, n_dec, dec_seq)),
        h_s[None],
        by_batch(xl_s)[:, dec_seq - tail:][None],
    )
```

```python
import functools

import jax
import jax.numpy as jnp
from jax import lax
from jax.experimental import pallas as pl
from jax.experimental.pallas import tpu as pltpu

F32 = jnp.float32
BF16 = jnp.bfloat16

D_MODEL = 1024
N_META = 16
ATTN_WIDTH = 512
LRU_WIDTH = 512
HEAD_DIM = 64
N_HEADS = 8
LRU_BLOCKS = 8
LRU_C = 8.0
CONV_WIDTH = 4
D_FF = 4096
RMS_EPS = 1e-6
RSQRT_FLOOR = 1e-30
PROJ_WIDTH = 3 * ATTN_WIDTH + 2 * LRU_WIDTH

SUBLANES = 8
LANES = 128
ROWS = 256
HEAD_PAIR = 2 * HEAD_DIM
HEAD_SLOT = 128
BIAS_PIECES = 3
LOG2E = 1.4426950408889634
Q_SCALE = HEAD_DIM ** -0.5 * LOG2E
Z2_MAX = 126.0
FF_CHUNK = 1024
MLP_ROWS = 512
SAMPLE_SEQS = 2
MIB = 1024 * 1024


def _rmsnorm(x, g):
    ms = jnp.mean(x * x, axis=-1, keepdims=True)
    return x * lax.rsqrt(ms + RMS_EPS) * g


def _gelu_tanh(x):
    return x * (0.5 * (1.0 + jnp.tanh(0.7978845608028654 * (x + 0.044715 * (x * x * x)))))


def _sigmoid(x):
    return 1.0 / (1.0 + jnp.exp(-x))


def _softplus(x):
    return jnp.maximum(x, 0.0) + jnp.log(1.0 + jnp.exp(-jnp.abs(x)))


def _inproj(x, g, w):
    hn = _rmsnorm(x, g).astype(BF16)
    return jnp.dot(hn, w, preferred_element_type=F32)


def _pad_heads(x):
    zeros = jnp.zeros((x.shape[0], HEAD_SLOT - HEAD_DIM), x.dtype)
    pieces = []
    for h in range(x.shape[1] // HEAD_DIM):
        pieces += [x[:, h * HEAD_DIM:(h + 1) * HEAD_DIM], zeros]
    return jnp.concatenate(pieces, axis=1)


def _inproj_prompt_kernel(x_ref, meta_ref, g_ref, w_ref, qx_ref, kx_ref, cw_ref, cb_ref, wg_ref, bg_ref, lam_ref,
                          q_ref, kt_ref, vb_ref, k_ref, v_ref, m_ref, hlast_ref, tail_out_ref,
                          h_ref, tail_ref, *, first_row):
    blk = pl.program_id(1)

    @pl.when(blk == 0)
    def _():
        h_ref[...] = jnp.zeros_like(h_ref)
        tail_ref[0] = jnp.zeros(tail_ref.shape[1:], F32)

    x = jnp.where(blk == 0, meta_ref[...], x_ref[0])
    hn = _rmsnorm(x, g_ref[...]).astype(BF16)

    lru_in = jnp.dot(hn, w_ref[:, 3 * ATTN_WIDTH:], preferred_element_type=F32)
    xl = lru_in[:, :LRU_WIDTH]
    xc = _lru_conv(xl, tail_ref[blk % 2], cw_ref, cb_ref[...])
    gates = jnp.dot(xc.astype(BF16), wg_ref[...], preferred_element_type=F32) + bg_ref[...]

    qkv = jnp.dot(hn, w_ref[:, :3 * ATTN_WIDTH], preferred_element_type=F32)
    k = qkv[:, ATTN_WIDTH:2 * ATTN_WIDTH]
    v = qkv[:, 2 * ATTN_WIDTH:]
    q_ref[0] = (_pad_heads(qkv[:, :ATTN_WIDTH] * Q_SCALE) + qx_ref[...]).astype(BF16)
    kt_ref[0] = (_pad_heads(k) + kx_ref[...]).T.astype(BF16)
    vb_ref[0] = v.astype(BF16)
    k_ref[0] = k
    v_ref[0] = v

    m, h = _lru_scan(xc, gates, lru_in[:, LRU_WIDTH:], h_ref[...], blk * ROWS - first_row, lam_ref[...])
    tail_ref[(blk + 1) % 2] = xl[ROWS - SUBLANES:]
    tail_out_ref[0] = xl[ROWS - SUBLANES:]
    h_ref[...] = h
    hlast_ref[0] = h
    m_ref[0] = m


def _inproj_sample_kernel(x_ref, g_ref, w_ref, q_ref, k_ref, v_ref, xl_ref, gate_ref):
    proj = _inproj(x_ref[...], g_ref[...], w_ref[...])
    q_ref[...] = (proj[:, :ATTN_WIDTH] * Q_SCALE).astype(BF16)
    k_ref[...] = proj[:, ATTN_WIDTH:2 * ATTN_WIDTH]
    v_ref[...] = proj[:, 2 * ATTN_WIDTH:3 * ATTN_WIDTH]
    xl_ref[...] = proj[:, 3 * ATTN_WIDTH:3 * ATTN_WIDTH + LRU_WIDTH]
    gate_ref[...] = proj[:, 3 * ATTN_WIDTH + LRU_WIDTH:]


def _const_spec(shape):
    return pl.BlockSpec(shape, lambda *_: (0,) * len(shape))


def _inproj_prompt(x_prompt, meta_blk, g, w_in, q_extra, k_extra, conv_w, conv_b, wg, bg, lam, n_blk, first_row):
    batch, _, _ = x_prompt.shape
    t_pad = n_blk * ROWS
    slots = N_HEADS * HEAD_SLOT

    def rows(width, dtype):
        return jax.ShapeDtypeStruct((batch, t_pad, width), dtype)

    row_spec = pl.BlockSpec((1, ROWS, ATTN_WIDTH), lambda b, i: (b, i, 0))
    return pl.pallas_call(
        functools.partial(_inproj_prompt_kernel, first_row=first_row),
        grid=(batch, n_blk),
        in_specs=[
            pl.BlockSpec((1, ROWS, D_MODEL), lambda b, i: (b, jnp.maximum(i - 1, 0), 0)),
            _const_spec((ROWS, D_MODEL)),
            _const_spec((1, D_MODEL)),
            _const_spec((D_MODEL, PROJ_WIDTH)),
            _const_spec((1, slots)),
            _const_spec((1, slots)),
            _const_spec((CONV_WIDTH, LRU_WIDTH)), _const_spec((1, LRU_WIDTH)),
            _const_spec((LRU_WIDTH, 2 * LRU_WIDTH)), _const_spec((1, 2 * LRU_WIDTH)),
            _const_spec((1, LRU_WIDTH)),
        ],
        out_specs=[
            pl.BlockSpec((1, ROWS, slots), lambda b, i: (b, i, 0)),
            pl.BlockSpec((1, slots, ROWS), lambda b, i: (b, 0, i)),
            row_spec, row_spec, row_spec,
            pl.BlockSpec((1, ROWS, LRU_WIDTH), lambda b, i: (b, jnp.maximum(i - 1, 0), 0)),
            pl.BlockSpec((1, 1, LRU_WIDTH), lambda b, i: (b, 0, 0)),
            pl.BlockSpec((1, SUBLANES, LRU_WIDTH), lambda b, i: (b, 0, 0)),
        ],
        out_shape=[
            rows(slots, BF16),
            jax.ShapeDtypeStruct((batch, slots, t_pad), BF16),
            rows(ATTN_WIDTH, BF16), rows(ATTN_WIDTH, F32), rows(ATTN_WIDTH, F32),
            jax.ShapeDtypeStruct((batch, t_pad - ROWS, LRU_WIDTH), BF16),
            jax.ShapeDtypeStruct((batch, 1, LRU_WIDTH), F32),
            jax.ShapeDtypeStruct((batch, SUBLANES, LRU_WIDTH), F32),
        ],
        scratch_shapes=[pltpu.VMEM((1, LRU_WIDTH), F32), pltpu.VMEM((2, SUBLANES, LRU_WIDTH), F32)],
        compiler_params=pltpu.CompilerParams(
            dimension_semantics=("arbitrary", "arbitrary"), vmem_limit_bytes=48 * MIB),
        name="inproj_prompt",
    )(x_prompt, meta_blk, g, w_in, q_extra, k_extra, conv_w, conv_b, wg, bg, lam)


def _inproj_sample(x, g, w_in):
    n = x.shape[0]
    row_spec = pl.BlockSpec((ROWS, ATTN_WIDTH), lambda i: (i, 0))
    f32_rows = jax.ShapeDtypeStruct((n, ATTN_WIDTH), F32)
    return pl.pallas_call(
        _inproj_sample_kernel,
        grid=(n // ROWS,),
        in_specs=[
            pl.BlockSpec((ROWS, D_MODEL), lambda i: (i, 0)),
            _const_spec((1, D_MODEL)),
            _const_spec((D_MODEL, PROJ_WIDTH)),
        ],
        out_specs=[row_spec] * 5,
        out_shape=[jax.ShapeDtypeStruct((n, ATTN_WIDTH), BF16), f32_rows, f32_rows, f32_rows, f32_rows],
        compiler_params=pltpu.CompilerParams(
            dimension_semantics=("arbitrary",), vmem_limit_bytes=48 * MIB),
        name="inproj_sample",
    )(x, g, w_in)


def _lru_coeffs(xc, gates, lam):
    r = _sigmoid(gates[:, :LRU_WIDTH])
    i = _sigmoid(gates[:, LRU_WIDTH:])
    log_a = (-LRU_C * _softplus(-lam)) * r
    a = jnp.exp(log_a)
    th = jnp.tanh(log_a)
    num = -2.0 * th
    b = num * lax.rsqrt(jnp.maximum(num * (1.0 - th), RSQRT_FLOOR)) * (i * xc)
    return a, b


def _lru_conv(xl, tail, cw_ref, cb):
    ext = jnp.concatenate([tail, xl], axis=0)
    xc = cb + jnp.zeros_like(xl)
    for j in range(CONV_WIDTH):
        shift = CONV_WIDTH - 1 - j
        shifted = xl if shift == 0 else pltpu.roll(ext, shift, 0)[SUBLANES:]
        xc = xc + shifted * cw_ref[j:j + 1, :]
    return xc


def _lru_scan(xc, gates, gate, h, rows_before, lam):
    a, b = _lru_coeffs(xc, gates, lam)
    row = lax.broadcasted_iota(jnp.int32, (ROWS, LRU_WIDTH), 0)
    b = jnp.where(rows_before + row >= 0, b, 0.0)

    groups = ROWS // SUBLANES
    a = a.reshape(groups, SUBLANES, LRU_WIDTH)
    b = b.reshape(groups, SUBLANES, LRU_WIDTH)
    sub = lax.broadcasted_iota(jnp.int32, a.shape, 1)
    for s in (1, 2, 4):
        keep = sub >= s
        b = jnp.where(keep, a * pltpu.roll(b, s, 1) + b, b)
        a = jnp.where(keep, a * pltpu.roll(a, s, 1), a)
    hs = []
    for g in range(groups):
        hg = a[g] * h + b[g]
        hs.append(hg)
        h = hg[SUBLANES - 1:]
    return (jnp.concatenate(hs, axis=0) * _gelu_tanh(gate)).astype(BF16), h


def _lru_sample_kernel(xl_ref, gate_ref, prev_ref, h0_ref, cw_ref, cb_ref, wg_ref, bg_ref, lam_ref,
                       m_ref, hlast_ref):
    steps = xl_ref.shape[0]
    xs = [prev_ref[j] for j in range(CONV_WIDTH - 1)] + [xl_ref[t] for t in range(steps)]
    xcs = []
    for t in range(steps):
        xc = cb_ref[...] + jnp.zeros_like(xs[0])
        for j in range(CONV_WIDTH):
            xc = xc + xs[t + j] * cw_ref[j:j + 1, :]
        xcs.append(xc)
    xc = jnp.concatenate(xcs, axis=0)
    gates = jnp.dot(xc.astype(BF16), wg_ref[...], preferred_element_type=F32) + bg_ref[...]
    a, b = _lru_coeffs(xc, gates, lam_ref[...])
    n = xs[0].shape[0]
    h = h0_ref[...]
    for t in range(steps):
        h = a[t * n:(t + 1) * n] * h + b[t * n:(t + 1) * n]
        m_ref[t] = (h * _gelu_tanh(gate_ref[t])).astype(BF16)
    hlast_ref[...] = h


def _lru_sample(xl_t, gate_t, prev_t, h0, conv_w, conv_b, wg, bg, lam):
    steps, n, _ = xl_t.shape
    return pl.pallas_call(
        _lru_sample_kernel,
        out_shape=[jax.ShapeDtypeStruct((steps, n, LRU_WIDTH), BF16),
                   jax.ShapeDtypeStruct((n, LRU_WIDTH), F32)],
        compiler_params=pltpu.CompilerParams(vmem_limit_bytes=32 * MIB),
        name="lru_sample",
    )(xl_t, gate_t, prev_t, h0, conv_w, conv_b, wg, bg, lam)


def _sb_clamp(z2):
    return lax.clamp(-Z2_MAX, z2, Z2_MAX)


def _sb_softplus(zc, mask):
    sp = jnp.log(1.0 + jnp.exp2(zc))
    return sp if mask is None else jnp.where(mask, sp, 0.0)


def _sb_weight(zc, sp, later, carry, mask):
    total = sp + later + carry
    w = jnp.exp2(zc - LOG2E * total)
    return (w if mask is None else jnp.where(mask, w, 0.0)).astype(BF16), total[:, :1]


def _attn_prompt_kernel(q_ref, kt_ref, v_ref, u_ref, o_ref, acc_ref, carry_ref, z_ref, w_ref, state_ref, *,
                        first_row):
    qi = pl.program_id(2)
    n_blk = pl.num_programs(2)
    heads = range(2)
    u = u_ref[...]
    acc_ref[...] = jnp.zeros_like(acc_ref)
    carry_ref[...] = jnp.zeros_like(carry_ref)
    row = lax.broadcasted_iota(jnp.int32, (ROWS, ROWS), 0)
    col = lax.broadcasted_iota(jnp.int32, (ROWS, ROWS), 1)

    def tile_start(kj):
        return kj * ROWS if isinstance(kj, int) else pl.multiple_of(kj * ROWS, ROWS)

    def logits(e, qj, kj):
        slot = slice(e * HEAD_SLOT, (e + 1) * HEAD_SLOT)
        return _sb_clamp(jnp.dot(q_ref[0, pl.ds(tile_start(qj), ROWS), slot],
                                 kt_ref[0, slot, pl.ds(tile_start(kj), ROWS)], preferred_element_type=F32))

    def step(kjs, masked, last, consume_prev=True):
        cols = [slice(first_row // LANES * LANES if isinstance(kj, int) and kj == 0 else 0, ROWS) for kj in kjs]
        masks = [((kj * ROWS + col >= first_row) & (kj * ROWS + col < qi * ROWS + row))[:, c] if m else None
                 for kj, m, c in zip(kjs, masked, cols)]
        cur = state_ref[1]
        nxt = 1 - cur
        carries = [carry_ref[e] for e in heads]
        accs = [acc_ref[e] for e in heads]
        zs = [[z_ref[cur, e, s, :, cols[s]] for s in range(len(kjs))] for e in heads]
        w_prev = [w_ref[cur, e] for e in heads]
        v_prev = v_ref[0, pl.ds(pl.multiple_of(state_ref[0] * ROWS, ROWS), 2 * ROWS), :]
        if last:
            q_ahead = jnp.minimum(qi + 1, n_blk - 1)
            k_ahead = [q_ahead, q_ahead - 1]
        else:
            q_ahead = qi
            k_ahead = [jnp.maximum(kjs[-1] - 1, 0), jnp.maximum(kjs[-1] - 2, 0)]

        pvs = [None for _ in heads]
        z_next = [[None, None] for _ in heads]

        def pv_job(e):
            def run():
                pvs[e] = jnp.dot(w_prev[e], v_prev, preferred_element_type=F32)
            return run

        def z_job(e, s):
            def run():
                z_next[e][s] = logits(e, q_ahead, k_ahead[s])
            return run

        fill = [pv_job(e) for e in heads] if consume_prev else []
        fill += [z_job(e, s) for s in range(2) for e in heads]
        chains = [(t, e) for t in range(len(kjs)) for e in heads]
        sps = [[None] * len(kjs) for _ in heads]
        laters = [[None] * len(kjs) for _ in heads]
        for n, (t, e) in enumerate(chains):
            sps[e][t] = _sb_softplus(zs[e][t], masks[t])
            width = sps[e][t].shape[1]
            laters[e][t] = jnp.dot(sps[e][t].astype(BF16), u[:width, :width], preferred_element_type=F32)
            for job in fill[n * len(fill) // len(chains):(n + 1) * len(fill) // len(chains)]:
                job()
        ws = [[] for _ in heads]
        for t, mask in enumerate(masks):
            for e in heads:
                w, carries[e] = _sb_weight(zs[e][t], sps[e][t], laters[e][t], carries[e], mask)
                if cols[t].start:
                    w = jnp.concatenate([jnp.zeros((ROWS, cols[t].start), BF16), w], axis=1)
                ws[e].append(w)
        for e in heads:
            carry_ref[e] = carries[e]
            acc = accs[e] + pvs[e] if consume_prev else accs[e]
            if len(kjs) == 2:
                w_ref[nxt, e] = jnp.concatenate(ws[e][::-1], axis=1)
            else:
                acc = acc + jnp.dot(ws[e][0], v_ref[0, pl.ds(tile_start(kjs[0]), ROWS), :],
                                    preferred_element_type=F32)
            acc_ref[e] = acc
            for s in range(2):
                z_ref[nxt, e, s] = z_next[e][s]
        if len(kjs) == 2:
            state_ref[0] = jnp.asarray(kjs[-1], jnp.int32)
        state_ref[1] = nxt

    @pl.when(qi == 0)
    def _():
        state_ref[0] = 0
        state_ref[1] = 0
        for e in heads:
            for s in range(2):
                z_ref[0, e, s] = logits(e, 0, 0)

    odd = qi % 2
    n_plain_pairs = (qi + 1) // 2 - 1 - odd

    @pl.when(qi == 1)
    def _():
        step([1, 0], [True, True], last=True, consume_prev=False)

    @pl.when(qi >= 2)
    def _():
        step([qi, qi - 1], [True, False], last=False, consume_prev=False)

    def plain_pair(t, _):
        kj = qi - 2 * (t + 1)
        step([kj, kj - 1], [False, False], last=False)
        return 0

    lax.fori_loop(0, n_plain_pairs, plain_pair, 0)

    @pl.when(jnp.logical_and(odd == 1, qi >= 3))
    def _():
        step([1, 0], [False, True], last=True)

    @pl.when(qi == 0)
    def _():
        step([0], [True], last=True, consume_prev=False)

    @pl.when(jnp.logical_and(odd == 0, qi >= 2))
    def _():
        step([0], [True], last=True)

    @pl.when(odd == 1)
    def _():
        for e in heads:
            acc_ref[e] += jnp.dot(w_ref[state_ref[1], e], v_ref[0, 0:2 * ROWS, :], preferred_element_type=F32)

    lane = lax.broadcasted_iota(jnp.int32, (ROWS, HEAD_PAIR), 1)
    o_ref[0] = jnp.where(lane < HEAD_DIM, acc_ref[0], acc_ref[1]).astype(BF16)


def _attn_prompt(q, kt, v, u, first_row):
    batch, t_pad, _ = v.shape
    n_blk = t_pad // ROWS
    return pl.pallas_call(
        functools.partial(_attn_prompt_kernel, first_row=first_row),
        grid=(batch, N_HEADS // 2, n_blk),
        in_specs=[
            pl.BlockSpec((1, t_pad, 2 * HEAD_SLOT), lambda b, p, i: (b, 0, p)),
            pl.BlockSpec((1, 2 * HEAD_SLOT, t_pad), lambda b, p, i: (b, p, 0)),
            pl.BlockSpec((1, t_pad, HEAD_PAIR), lambda b, p, i: (b, 0, p)),
            _const_spec((ROWS, ROWS)),
        ],
        out_specs=pl.BlockSpec((1, ROWS, HEAD_PAIR), lambda b, p, i: (b, jnp.maximum(i - 1, 0), p)),
        out_shape=jax.ShapeDtypeStruct((batch, t_pad - ROWS, ATTN_WIDTH), BF16),
        scratch_shapes=[
            pltpu.VMEM((2, ROWS, HEAD_PAIR), F32),
            pltpu.VMEM((2, ROWS, 1), F32),
            pltpu.VMEM((2, 2, 2, ROWS, ROWS), F32),
            pltpu.VMEM((2, 2, ROWS, 2 * ROWS), BF16),
            pltpu.SMEM((2,), jnp.int32),
        ],
        compiler_params=pltpu.CompilerParams(
            dimension_semantics=("arbitrary", "arbitrary", "arbitrary"), vmem_limit_bytes=40 * MIB),
        name="attn_prompt",
    )(q, kt, v, u)


def _attn_sample_kernel(pt_ref, bias_ref, q_ref, kn_ref, vn_ref, u_ref, *refs, n_pages):
    del pt_ref
    seqs = q_ref.shape[0]
    k_refs = refs[:seqs * n_pages]
    v_refs = refs[seqs * n_pages:2 * seqs * n_pages]
    o_ref = refs[2 * seqs * n_pages]
    steps = q_ref.shape[1]
    page = k_refs[0].shape[-1]
    rows = N_HEADS * steps
    contract_lanes = (((1,), (1,)), ((), ()))

    row_head = lax.broadcasted_iota(jnp.int32, (rows, ATTN_WIDTH), 0) // steps
    lane_head = lax.broadcasted_iota(jnp.int32, (rows, ATTN_WIDTH), 1) // HEAD_DIM
    key_row = lax.broadcasted_iota(jnp.int32, (rows, page), 0)
    bias = jnp.zeros((rows, page), F32)
    for h in range(N_HEADS):
        bias = jnp.where(key_row // steps == h, bias_ref[h], bias)
    new_mask = lax.broadcasted_iota(jnp.int32, (rows, page), 1) < key_row % steps
    masks = [new_mask] + [None] * n_pages
    out_lane_head = lax.broadcasted_iota(jnp.int32, (steps, ATTN_WIDTH), 1) // HEAD_DIM
    pad = jnp.zeros((page - steps, ATTN_WIDTH), F32)

    for j in range(seqs):
        k_pages = k_refs[j * n_pages:(j + 1) * n_pages]
        v_pages = v_refs[j * n_pages:(j + 1) * n_pages]
        q_all = jnp.concatenate([q_ref[j].astype(F32)] * N_HEADS, axis=0)
        q_bd = jnp.where(row_head == lane_head, q_all, 0.0).astype(BF16)

        k_new = jnp.concatenate([kn_ref[j], pad], axis=0).astype(BF16)
        v_new = jnp.concatenate([vn_ref[j], pad], axis=0).astype(BF16)
        zs = [_sb_clamp(lax.dot_general(q_bd, k_new, contract_lanes, preferred_element_type=F32) + bias)]
        for p in reversed(range(n_pages)):
            kt = k_pages[p][...].reshape(ATTN_WIDTH, page).astype(BF16)
            zs.append(_sb_clamp(jnp.dot(q_bd, kt, preferred_element_type=F32) + bias))
        sps = [_sb_softplus(z, mask) for z, mask in zip(zs, masks)]

        later = jnp.dot(jnp.concatenate(sps, axis=0).astype(BF16), u_ref[...], preferred_element_type=F32)
        carry = jnp.zeros((rows, 1), F32)
        ws = []
        for n, (zc, sp, mask) in enumerate(zip(zs, sps, masks)):
            w, carry = _sb_weight(zc, sp, later[n * rows:(n + 1) * rows], carry, mask)
            ws.append(w)

        acc = jnp.dot(ws[0], v_new, preferred_element_type=F32)
        for w, p in zip(ws[1:], reversed(range(n_pages))):
            vt = v_pages[p][...].reshape(ATTN_WIDTH, page).astype(BF16)
            acc = acc + lax.dot_general(w, vt, contract_lanes, preferred_element_type=F32)

        out = jnp.zeros((steps, ATTN_WIDTH), F32)
        for h in range(N_HEADS):
            out = jnp.where(out_lane_head == h, acc[h * steps:(h + 1) * steps], out)
        o_ref[j] = out


def _attn_sample(page_table, bias2, q, k_new, v_new, u, cache_kt, cache_vt):
    n, steps, _ = q.shape
    n_pages = page_table.shape[1]
    page = cache_kt.shape[-1]
    tok_spec = pl.BlockSpec((SAMPLE_SEQS, steps, ATTN_WIDTH), lambda b, pt: (b, 0, 0))

    def page_spec(j, p):
        return pl.BlockSpec((None, None, N_HEADS, HEAD_DIM, page),
                            lambda b, pt: (0, pt[SAMPLE_SEQS * b + j, p], 0, 0, 0))

    page_specs = [page_spec(j, p) for j in range(SAMPLE_SEQS) for p in range(n_pages)]
    grid_spec = pltpu.PrefetchScalarGridSpec(
        num_scalar_prefetch=1,
        grid=(n // SAMPLE_SEQS,),
        in_specs=[pl.BlockSpec(memory_space=pltpu.SMEM), tok_spec, tok_spec, tok_spec,
                  pl.BlockSpec((page, page), lambda b, pt: (0, 0))] + page_specs * 2,
        out_specs=tok_spec,
    )
    n_page_args = SAMPLE_SEQS * n_pages
    return pl.pallas_call(
        functools.partial(_attn_sample_kernel, n_pages=n_pages),
        grid_spec=grid_spec,
        out_shape=jax.ShapeDtypeStruct((n, steps, ATTN_WIDTH), F32),
        compiler_params=pltpu.CompilerParams(
            dimension_semantics=("arbitrary",), vmem_limit_bytes=56 * MIB),
        name="attn_sample",
    )(page_table, bias2, q, k_new, v_new, u, *([cache_kt] * n_page_args), *([cache_vt] * n_page_args))


def _outproj_mlp_kernel(attn_ref, lru_ref, x_ref, wo_ref, gpost_ref, gpre_ref, gout_ref, wu_ref, wd_ref,
                        y_ref):
    mixed = (jnp.dot(attn_ref[0].astype(BF16), wo_ref[:ATTN_WIDTH, :], preferred_element_type=F32)
             + jnp.dot(lru_ref[0], wo_ref[ATTN_WIDTH:, :], preferred_element_type=F32))
    x = x_ref[0] + _rmsnorm(mixed, gpost_ref[...])
    hn = _rmsnorm(x, gpre_ref[...]).astype(BF16)
    acc = jnp.zeros_like(x)
    for c in range(D_FF // FF_CHUNK):
        cols = slice(c * FF_CHUNK, (c + 1) * FF_CHUNK)
        up = jnp.maximum(jnp.dot(hn, wu_ref[:, cols], preferred_element_type=F32), 0.0)
        acc = acc + jnp.dot((up * up).astype(BF16), wd_ref[cols, :], preferred_element_type=F32)
    y_ref[0] = x + _rmsnorm(acc, gout_ref[...])


def _outproj_mlp(attn, lru, x, w_out, g_post, g_pre, g_out, w_up, w_down):
    batch, t, _ = x.shape
    mix_spec = pl.BlockSpec((1, MLP_ROWS, ATTN_WIDTH), lambda b, i: (b, i, 0))
    x_spec = pl.BlockSpec((1, MLP_ROWS, D_MODEL), lambda b, i: (b, i, 0))

    def weight_spec(shape):
        return pl.BlockSpec(shape, lambda b, i: (0, 0), pipeline_mode=pl.Buffered(1))

    return pl.pallas_call(
        _outproj_mlp_kernel,
        grid=(batch, t // MLP_ROWS),
        in_specs=[
            mix_spec, mix_spec, x_spec,
            weight_spec((D_MODEL, D_MODEL)),
            _const_spec((1, D_MODEL)), _const_spec((1, D_MODEL)), _const_spec((1, D_MODEL)),
            weight_spec((D_MODEL, D_FF)), weight_spec((D_FF, D_MODEL)),
        ],
        out_specs=x_spec,
        out_shape=jax.ShapeDtypeStruct((batch, t, D_MODEL), F32),
        compiler_params=pltpu.CompilerParams(
            dimension_semantics=("arbitrary", "arbitrary"), vmem_limit_bytes=56 * MIB),
        name="outproj_mlp",
    )(attn, lru, x, w_out, g_post, g_pre, g_out, w_up, w_down)


def _block_diag(w):
    n, c, d = w.shape
    eye = jnp.eye(n, dtype=w.dtype)
    return (eye[:, None, :, None] * w[:, :, None, :]).reshape(n * c, n * d)


def _later_keys_matrix(n):
    idx = jnp.arange(n)
    return (idx[:, None] > idx[None, :]).astype(BF16)


def _bias_lanes(bias2):
    pieces = []
    rest = bias2
    for _ in range(BIAS_PIECES):
        piece = rest.astype(BF16).astype(F32)
        pieces.append(piece)
        rest = rest - piece
    k_slot = jnp.zeros((N_HEADS, HEAD_SLOT), F32).at[:, HEAD_DIM:HEAD_DIM + BIAS_PIECES].set(
        jnp.stack(pieces, axis=1))
    q_slot = jnp.zeros((N_HEADS, HEAD_SLOT), F32).at[:, HEAD_DIM:HEAD_DIM + BIAS_PIECES].set(1.0)
    return q_slot.reshape(1, -1), k_slot.reshape(1, -1)


def kernel(x_prompt, x_sample, cache_k, cache_v, state_h, state_conv, page_table, meta_tokens,
           g_mix_pre, g_mix_post, g_mlp_pre, g_mlp_post, w_in, sb_bias, conv_w, conv_b, w_gate_a, b_gate_a,
           w_gate_x, b_gate_x, lru_lambda, w_out, w_up, w_down):
    depth = w_in.shape[0]
    assert depth == 1, "single-layer stack"
    batch, seq, _ = x_prompt.shape
    n_dec, dec_seq, _ = x_sample.shape
    assert seq % MLP_ROWS == 0 and (n_dec * dec_seq) % MLP_ROWS == 0 and dec_seq >= CONV_WIDTH - 1
    assert n_dec % SAMPLE_SEQS == 0

    first_row = ROWS - N_META
    n_blk = seq // ROWS + 1
    t_pad = n_blk * ROWS
    meta_blk = jnp.zeros((ROWS, D_MODEL), F32).at[first_row:].set(meta_tokens.astype(F32))

    row = lambda a: a.reshape(1, -1)
    w_in_b = w_in[0].astype(BF16)
    w_out_b = w_out[0].astype(BF16)
    w_up_b = w_up[0].astype(BF16)
    w_down_b = w_down[0].astype(BF16)
    wg = jnp.concatenate([_block_diag(w_gate_a[0]), _block_diag(w_gate_x[0])], axis=1).astype(BF16)
    bg = jnp.concatenate([b_gate_a[0], b_gate_x[0]]).reshape(1, -1)
    lru_args = (conv_w[0], row(conv_b[0]), wg, bg, row(lru_lambda[0]))
    mlp_args = (w_out_b, row(g_mix_post[0]), row(g_mlp_pre[0]), row(g_mlp_post[0]), w_up_b, w_down_b)
    bias2 = sb_bias[0].astype(F32) * LOG2E
    q_extra, k_extra = _bias_lanes(bias2)

    q_p, kt_p, vb_p, k_p, v_p, m_p, h_p, xl_tail_p = _inproj_prompt(
        x_prompt, meta_blk, row(g_mix_pre[0]), w_in_b, q_extra, k_extra, *lru_args, n_blk, first_row)
    attn_p = _attn_prompt(q_p, kt_p, vb_p, _later_keys_matrix(ROWS), first_row)
    y_prompt = _outproj_mlp(attn_p, m_p, x_prompt, *mlp_args)

    xs = x_sample.reshape(n_dec * dec_seq, D_MODEL)
    q_s, k_s, v_s, xl_s, gate_s = _inproj_sample(xs, row(g_mix_pre[0]), w_in_b)
    by_batch = lambda a: a.reshape(n_dec, dec_seq, -1)
    time_major = lambda a: by_batch(a).swapaxes(0, 1)
    m_s, h_s = _lru_sample(time_major(xl_s), time_major(gate_s), state_conv[0].swapaxes(0, 1), state_h[0],
                           *lru_args)
    feature_major = lambda c: jnp.transpose(c, (0, 1, 3, 4, 2))
    attn_s = _attn_sample(page_table, bias2, by_batch(q_s), by_batch(k_s), by_batch(v_s),
                          _later_keys_matrix(cache_k.shape[2]), feature_major(cache_k), feature_major(cache_v))
    y_sample = _outproj_mlp(attn_s.reshape(1, n_dec * dec_seq, ATTN_WIDTH),
                            m_s.swapaxes(0, 1).reshape(1, n_dec * dec_seq, LRU_WIDTH),
                            xs[None], *mlp_args)[0].reshape(x_sample.shape)

    heads = lambda a, lead: a.reshape(*lead, N_HEADS, HEAD_DIM)
    tail = CONV_WIDTH - 1
    return (
        y_prompt,
        y_sample,
        heads(k_p[:, first_row:], (1, batch, seq + N_META)),
        heads(v_p[:, first_row:], (1, batch, seq + N_META)),
        h_p.reshape(1, batch, LRU_WIDTH),
        xl_tail_p[:, SUBLANES - tail:][None],
        heads(k_s, (1, n_dec, dec_seq)),
        heads(v_s, (1, n_dec, dec_seq)),
        h_s[None],
        by_batch(xl_s)[:, dec_seq - tail:][None],
    )
```

```python
import functools

import jax
import jax.numpy as jnp
from jax import lax
from jax.experimental import pallas as pl
from jax.experimental.pallas import tpu as pltpu

F32 = jnp.float32
BF16 = jnp.bfloat16

D_MODEL = 1024
N_META = 16
ATTN_WIDTH = 512
LRU_WIDTH = 512
HEAD_DIM = 64
N_HEADS = 8
LRU_BLOCKS = 8
LRU_C = 8.0
CONV_WIDTH = 4
D_FF = 4096
RMS_EPS = 1e-6
RSQRT_FLOOR = 1e-30
PROJ_WIDTH = 3 * ATTN_WIDTH + 2 * LRU_WIDTH

SUBLANES = 8
LANES = 128
ROWS = 256
HEAD_PAIR = 2 * HEAD_DIM
HEAD_SLOT = 128
BIAS_PIECES = 3
LOG2E = 1.4426950408889634
Q_SCALE = HEAD_DIM ** -0.5 * LOG2E
Z2_MAX = 126.0
FF_CHUNK = 1024
MLP_ROWS = 512
SAMPLE_SEQS = 2
MIB = 1024 * 1024


def _rmsnorm(x, g):
    ms = jnp.mean(x * x, axis=-1, keepdims=True)
    return x * lax.rsqrt(ms + RMS_EPS) * g


def _gelu_tanh(x):
    return x * (0.5 * (1.0 + jnp.tanh(0.7978845608028654 * (x + 0.044715 * (x * x * x)))))


def _sigmoid(x):
    return 1.0 / (1.0 + jnp.exp(-x))


def _softplus(x):
    return jnp.maximum(x, 0.0) + jnp.log(1.0 + jnp.exp(-jnp.abs(x)))


def _inproj(x, g, w):
    hn = _rmsnorm(x, g).astype(BF16)
    return jnp.dot(hn, w, preferred_element_type=F32)


def _pad_heads(x):
    zeros = jnp.zeros((x.shape[0], HEAD_SLOT - HEAD_DIM), x.dtype)
    pieces = []
    for h in range(x.shape[1] // HEAD_DIM):
        pieces += [x[:, h * HEAD_DIM:(h + 1) * HEAD_DIM], zeros]
    return jnp.concatenate(pieces, axis=1)


def _inproj_prompt_kernel(x_ref, meta_ref, g_ref, w_ref, qx_ref, kx_ref, cw_ref, cb_ref, wg_ref, bg_ref, lam_ref,
                          q_ref, kt_ref, vb_ref, k_ref, v_ref, m_ref, hlast_ref, tail_out_ref,
                          h_ref, tail_ref, *, first_row):
    blk = pl.program_id(1)

    @pl.when(blk == 0)
    def _():
        h_ref[...] = jnp.zeros_like(h_ref)
        tail_ref[0] = jnp.zeros(tail_ref.shape[1:], F32)

    x = jnp.where(blk == 0, meta_ref[...], x_ref[0])
    hn = _rmsnorm(x, g_ref[...]).astype(BF16)

    lru_in = jnp.dot(hn, w_ref[:, 3 * ATTN_WIDTH:], preferred_element_type=F32)
    xl = lru_in[:, :LRU_WIDTH]
    xc = _lru_conv(xl, tail_ref[blk % 2], cw_ref, cb_ref[...])
    gates = jnp.dot(xc.astype(BF16), wg_ref[...], preferred_element_type=F32) + bg_ref[...]

    qkv = jnp.dot(hn, w_ref[:, :3 * ATTN_WIDTH], preferred_element_type=F32)
    k = qkv[:, ATTN_WIDTH:2 * ATTN_WIDTH]
    v = qkv[:, 2 * ATTN_WIDTH:]
    q_ref[0] = (_pad_heads(qkv[:, :ATTN_WIDTH] * Q_SCALE) + qx_ref[...]).astype(BF16)
    kt_ref[0] = (_pad_heads(k) + kx_ref[...]).T.astype(BF16)
    vb_ref[0] = v.astype(BF16)
    k_ref[0] = k
    v_ref[0] = v

    m, h = _lru_scan(xc, gates, lru_in[:, LRU_WIDTH:], h_ref[...], blk * ROWS - first_row, lam_ref[...])
    tail_ref[(blk + 1) % 2] = xl[ROWS - SUBLANES:]
    tail_out_ref[0] = xl[ROWS - SUBLANES:]
    h_ref[...] = h
    hlast_ref[0] = h
    m_ref[0] = m


def _inproj_sample_kernel(x_ref, g_ref, w_ref, q_ref, k_ref, v_ref, xl_ref, gate_ref):
    proj = _inproj(x_ref[...], g_ref[...], w_ref[...])
    q_ref[...] = (proj[:, :ATTN_WIDTH] * Q_SCALE).astype(BF16)
    k_ref[...] = proj[:, ATTN_WIDTH:2 * ATTN_WIDTH]
    v_ref[...] = proj[:, 2 * ATTN_WIDTH:3 * ATTN_WIDTH]
    xl_ref[...] = proj[:, 3 * ATTN_WIDTH:3 * ATTN_WIDTH + LRU_WIDTH]
    gate_ref[...] = proj[:, 3 * ATTN_WIDTH + LRU_WIDTH:]


def _const_spec(shape):
    return pl.BlockSpec(shape, lambda *_: (0,) * len(shape))


def _inproj_prompt(x_prompt, meta_blk, g, w_in, q_extra, k_extra, conv_w, conv_b, wg, bg, lam, n_blk, first_row):
    batch, _, _ = x_prompt.shape
    t_pad = n_blk * ROWS
    slots = N_HEADS * HEAD_SLOT

    def rows(width, dtype):
        return jax.ShapeDtypeStruct((batch, t_pad, width), dtype)

    row_spec = pl.BlockSpec((1, ROWS, ATTN_WIDTH), lambda b, i: (b, i, 0))
    return pl.pallas_call(
        functools.partial(_inproj_prompt_kernel, first_row=first_row),
        grid=(batch, n_blk),
        in_specs=[
            pl.BlockSpec((1, ROWS, D_MODEL), lambda b, i: (b, jnp.maximum(i - 1, 0), 0)),
            _const_spec((ROWS, D_MODEL)),
            _const_spec((1, D_MODEL)),
            _const_spec((D_MODEL, PROJ_WIDTH)),
            _const_spec((1, slots)),
            _const_spec((1, slots)),
            _const_spec((CONV_WIDTH, LRU_WIDTH)), _const_spec((1, LRU_WIDTH)),
            _const_spec((LRU_WIDTH, 2 * LRU_WIDTH)), _const_spec((1, 2 * LRU_WIDTH)),
            _const_spec((1, LRU_WIDTH)),
        ],
        out_specs=[
            pl.BlockSpec((1, ROWS, slots), lambda b, i: (b, i, 0)),
            pl.BlockSpec((1, slots, ROWS), lambda b, i: (b, 0, i)),
            row_spec, row_spec, row_spec,
            pl.BlockSpec((1, ROWS, LRU_WIDTH), lambda b, i: (b, jnp.maximum(i - 1, 0), 0)),
            pl.BlockSpec((1, 1, LRU_WIDTH), lambda b, i: (b, 0, 0)),
            pl.BlockSpec((1, SUBLANES, LRU_WIDTH), lambda b, i: (b, 0, 0)),
        ],
        out_shape=[
            rows(slots, BF16),
            jax.ShapeDtypeStruct((batch, slots, t_pad), BF16),
            rows(ATTN_WIDTH, BF16), rows(ATTN_WIDTH, F32), rows(ATTN_WIDTH, F32),
            jax.ShapeDtypeStruct((batch, t_pad - ROWS, LRU_WIDTH), BF16),
            jax.ShapeDtypeStruct((batch, 1, LRU_WIDTH), F32),
            jax.ShapeDtypeStruct((batch, SUBLANES, LRU_WIDTH), F32),
        ],
        scratch_shapes=[pltpu.VMEM((1, LRU_WIDTH), F32), pltpu.VMEM((2, SUBLANES, LRU_WIDTH), F32)],
        compiler_params=pltpu.CompilerParams(
            dimension_semantics=("arbitrary", "arbitrary"), vmem_limit_bytes=48 * MIB),
        name="inproj_prompt",
    )(x_prompt, meta_blk, g, w_in, q_extra, k_extra, conv_w, conv_b, wg, bg, lam)


def _inproj_sample(x, g, w_in):
    n = x.shape[0]
    row_spec = pl.BlockSpec((ROWS, ATTN_WIDTH), lambda i: (i, 0))
    f32_rows = jax.ShapeDtypeStruct((n, ATTN_WIDTH), F32)
    return pl.pallas_call(
        _inproj_sample_kernel,
        grid=(n // ROWS,),
        in_specs=[
            pl.BlockSpec((ROWS, D_MODEL), lambda i: (i, 0)),
            _const_spec((1, D_MODEL)),
            _const_spec((D_MODEL, PROJ_WIDTH)),
        ],
        out_specs=[row_spec] * 5,
        out_shape=[jax.ShapeDtypeStruct((n, ATTN_WIDTH), BF16), f32_rows, f32_rows, f32_rows, f32_rows],
        compiler_params=pltpu.CompilerParams(
            dimension_semantics=("arbitrary",), vmem_limit_bytes=48 * MIB),
        name="inproj_sample",
    )(x, g, w_in)


def _lru_coeffs(xc, gates, lam):
    r = _sigmoid(gates[:, :LRU_WIDTH])
    i = _sigmoid(gates[:, LRU_WIDTH:])
    log_a = (-LRU_C * _softplus(-lam)) * r
    a = jnp.exp(log_a)
    th = jnp.tanh(log_a)
    num = -2.0 * th
    b = num * lax.rsqrt(jnp.maximum(num * (1.0 - th), RSQRT_FLOOR)) * (i * xc)
    return a, b


def _lru_conv(xl, tail, cw_ref, cb):
    ext = jnp.concatenate([tail, xl], axis=0)
    xc = cb + jnp.zeros_like(xl)
    for j in range(CONV_WIDTH):
        shift = CONV_WIDTH - 1 - j
        shifted = xl if shift == 0 else pltpu.roll(ext, shift, 0)[SUBLANES:]
        xc = xc + shifted * cw_ref[j:j + 1, :]
    return xc


def _lru_scan(xc, gates, gate, h, rows_before, lam):
    a, b = _lru_coeffs(xc, gates, lam)
    row = lax.broadcasted_iota(jnp.int32, (ROWS, LRU_WIDTH), 0)
    b = jnp.where(rows_before + row >= 0, b, 0.0)

    groups = ROWS // SUBLANES
    a = a.reshape(groups, SUBLANES, LRU_WIDTH)
    b = b.reshape(groups, SUBLANES, LRU_WIDTH)
    sub = lax.broadcasted_iota(jnp.int32, a.shape, 1)
    for s in (1, 2, 4):
        keep = sub >= s
        b = jnp.where(keep, a * pltpu.roll(b, s, 1) + b, b)
        a = jnp.where(keep, a * pltpu.roll(a, s, 1), a)
    hs = []
    for g in range(groups):
        hg = a[g] * h + b[g]
        hs.append(hg)
        h = hg[SUBLANES - 1:]
    return (jnp.concatenate(hs, axis=0) * _gelu_tanh(gate)).astype(BF16), h


def _lru_sample_kernel(xl_ref, gate_ref, prev_ref, h0_ref, cw_ref, cb_ref, wg_ref, bg_ref, lam_ref,
                       m_ref, hlast_ref):
    steps = xl_ref.shape[0]
    xs = [prev_ref[j] for j in range(CONV_WIDTH - 1)] + [xl_ref[t] for t in range(steps)]
    xcs = []
    for t in range(steps):
        xc = cb_ref[...] + jnp.zeros_like(xs[0])
        for j in range(CONV_WIDTH):
            xc = xc + xs[t + j] * cw_ref[j:j + 1, :]
        xcs.append(xc)
    xc = jnp.concatenate(xcs, axis=0)
    gates = jnp.dot(xc.astype(BF16), wg_ref[...], preferred_element_type=F32) + bg_ref[...]
    a, b = _lru_coeffs(xc, gates, lam_ref[...])
    n = xs[0].shape[0]
    h = h0_ref[...]
    for t in range(steps):
        h = a[t * n:(t + 1) * n] * h + b[t * n:(t + 1) * n]
        m_ref[t] = (h * _gelu_tanh(gate_ref[t])).astype(BF16)
    hlast_ref[...] = h


def _lru_sample(xl_t, gate_t, prev_t, h0, conv_w, conv_b, wg, bg, lam):
    steps, n, _ = xl_t.shape
    return pl.pallas_call(
        _lru_sample_kernel,
        out_shape=[jax.ShapeDtypeStruct((steps, n, LRU_WIDTH), BF16),
                   jax.ShapeDtypeStruct((n, LRU_WIDTH), F32)],
        compiler_params=pltpu.CompilerParams(vmem_limit_bytes=32 * MIB),
        name="lru_sample",
    )(xl_t, gate_t, prev_t, h0, conv_w, conv_b, wg, bg, lam)


def _sb_clamp(z2):
    return lax.clamp(-Z2_MAX, z2, Z2_MAX)


def _sb_softplus(zc, mask):
    sp = jnp.log(1.0 + jnp.exp2(zc))
    return sp if mask is None else jnp.where(mask, sp, 0.0)


def _sb_weight(zc, sp, later, carry, mask):
    total = sp + later + carry
    w = jnp.exp2(zc - LOG2E * total)
    return (w if mask is None else jnp.where(mask, w, 0.0)).astype(BF16), total[:, :1]


def _attn_prompt_kernel(*refs, first_row):
    n_blk = refs[0].shape[1] // ROWS

    def q_tile(qi, _):
        _attn_q_tile(qi, n_blk, *refs, first_row=first_row)
        return 0

    lax.fori_loop(0, n_blk, q_tile, 0)


def _attn_q_tile(qi, n_blk, q_ref, kt_ref, v_ref, u_ref, o_ref, acc_ref, carry_ref, z_ref, w_ref, state_ref, *,
                 first_row):
    heads = range(2)
    u = u_ref[...]
    acc_ref[...] = jnp.zeros_like(acc_ref)
    carry_ref[...] = jnp.zeros_like(carry_ref)
    row = lax.broadcasted_iota(jnp.int32, (ROWS, ROWS), 0)
    col = lax.broadcasted_iota(jnp.int32, (ROWS, ROWS), 1)

    def tile_start(kj):
        return kj * ROWS if isinstance(kj, int) else pl.multiple_of(kj * ROWS, ROWS)

    def logits(e, qj, kj):
        slot = slice(e * HEAD_SLOT, (e + 1) * HEAD_SLOT)
        return _sb_clamp(jnp.dot(q_ref[0, pl.ds(tile_start(qj), ROWS), slot],
                                 kt_ref[0, slot, pl.ds(tile_start(kj), ROWS)], preferred_element_type=F32))

    def step(kjs, masked, last, consume_prev=True):
        cols = [slice(first_row // LANES * LANES if isinstance(kj, int) and kj == 0 else 0, ROWS) for kj in kjs]
        masks = [((kj * ROWS + col >= first_row) & (kj * ROWS + col < qi * ROWS + row))[:, c] if m else None
                 for kj, m, c in zip(kjs, masked, cols)]
        cur = state_ref[1]
        nxt = 1 - cur
        carries = [carry_ref[e] for e in heads]
        accs = [acc_ref[e] for e in heads]
        zs = [[z_ref[cur, e, s, :, cols[s]] for s in range(len(kjs))] for e in heads]
        w_prev = [w_ref[cur, e] for e in heads]
        v_prev = v_ref[0, pl.ds(pl.multiple_of(state_ref[0] * ROWS, ROWS), 2 * ROWS), :]
        if last:
            q_ahead = jnp.minimum(qi + 1, n_blk - 1)
            k_ahead = [q_ahead, q_ahead - 1]
        else:
            q_ahead = qi
            k_ahead = [jnp.maximum(kjs[-1] - 1, 0), jnp.maximum(kjs[-1] - 2, 0)]

        pvs = [None for _ in heads]
        z_next = [[None, None] for _ in heads]

        def pv_job(e):
            def run():
                pvs[e] = jnp.dot(w_prev[e], v_prev, preferred_element_type=F32)
            return run

        def z_job(e, s):
            def run():
                z_next[e][s] = logits(e, q_ahead, k_ahead[s])
            return run

        fill = [pv_job(e) for e in heads] if consume_prev else []
        fill += [z_job(e, s) for s in range(2) for e in heads]
        chains = [(t, e) for t in range(len(kjs)) for e in heads]
        sps = [[None] * len(kjs) for _ in heads]
        laters = [[None] * len(kjs) for _ in heads]
        for n, (t, e) in enumerate(chains):
            sps[e][t] = _sb_softplus(zs[e][t], masks[t])
            width = sps[e][t].shape[1]
            laters[e][t] = jnp.dot(sps[e][t].astype(BF16), u[:width, :width], preferred_element_type=F32)
            for job in fill[n * len(fill) // len(chains):(n + 1) * len(fill) // len(chains)]:
                job()
        ws = [[] for _ in heads]
        for t, mask in enumerate(masks):
            for e in heads:
                w, carries[e] = _sb_weight(zs[e][t], sps[e][t], laters[e][t], carries[e], mask)
                if cols[t].start:
                    w = jnp.concatenate([jnp.zeros((ROWS, cols[t].start), BF16), w], axis=1)
                ws[e].append(w)
        for e in heads:
            carry_ref[e] = carries[e]
            acc = accs[e] + pvs[e] if consume_prev else accs[e]
            if len(kjs) == 2:
                w_ref[nxt, e] = jnp.concatenate(ws[e][::-1], axis=1)
            else:
                acc = acc + jnp.dot(ws[e][0], v_ref[0, pl.ds(tile_start(kjs[0]), ROWS), :],
                                    preferred_element_type=F32)
            acc_ref[e] = acc
            for s in range(2):
                z_ref[nxt, e, s] = z_next[e][s]
        if len(kjs) == 2:
            state_ref[0] = jnp.asarray(kjs[-1], jnp.int32)
        state_ref[1] = nxt

    @pl.when(qi == 0)
    def _():
        state_ref[0] = 0
        state_ref[1] = 0
        for e in heads:
            for s in range(2):
                z_ref[0, e, s] = logits(e, 0, 0)

    odd = qi % 2
    n_plain_pairs = (qi + 1) // 2 - 1 - odd

    @pl.when(qi == 1)
    def _():
        step([1, 0], [True, True], last=True, consume_prev=False)

    @pl.when(qi >= 2)
    def _():
        step([qi, qi - 1], [True, False], last=False, consume_prev=False)

    def plain_pair(t, _):
        kj = qi - 2 * (t + 1)
        step([kj, kj - 1], [False, False], last=False)
        return 0

    lax.fori_loop(0, n_plain_pairs, plain_pair, 0)

    @pl.when(jnp.logical_and(odd == 1, qi >= 3))
    def _():
        step([1, 0], [False, True], last=True)

    @pl.when(qi == 0)
    def _():
        step([0], [True], last=True, consume_prev=False)

    @pl.when(jnp.logical_and(odd == 0, qi >= 2))
    def _():
        step([0], [True], last=True)

    @pl.when(odd == 1)
    def _():
        for e in heads:
            acc_ref[e] += jnp.dot(w_ref[state_ref[1], e], v_ref[0, 0:2 * ROWS, :], preferred_element_type=F32)

    @pl.when(qi >= 1)
    def _():
        lane = lax.broadcasted_iota(jnp.int32, (ROWS, HEAD_PAIR), 1)
        o_ref[0, pl.ds(tile_start(qi - 1), ROWS), :] = (
            jnp.where(lane < HEAD_DIM, acc_ref[0], acc_ref[1]).astype(BF16))


def _attn_prompt(q, kt, v, u, first_row):
    batch, t_pad, _ = v.shape
    return pl.pallas_call(
        functools.partial(_attn_prompt_kernel, first_row=first_row),
        grid=(batch, N_HEADS // 2),
        in_specs=[
            pl.BlockSpec((1, t_pad, 2 * HEAD_SLOT), lambda b, p: (b, 0, p)),
            pl.BlockSpec((1, 2 * HEAD_SLOT, t_pad), lambda b, p: (b, p, 0)),
            pl.BlockSpec((1, t_pad, HEAD_PAIR), lambda b, p: (b, 0, p)),
            _const_spec((ROWS, ROWS)),
        ],
        out_specs=pl.BlockSpec((1, t_pad - ROWS, HEAD_PAIR), lambda b, p: (b, 0, p)),
        out_shape=jax.ShapeDtypeStruct((batch, t_pad - ROWS, ATTN_WIDTH), BF16),
        scratch_shapes=[
            pltpu.VMEM((2, ROWS, HEAD_PAIR), F32),
            pltpu.VMEM((2, ROWS, 1), F32),
            pltpu.VMEM((2, 2, 2, ROWS, ROWS), F32),
            pltpu.VMEM((2, 2, ROWS, 2 * ROWS), BF16),
            pltpu.SMEM((2,), jnp.int32),
        ],
        compiler_params=pltpu.CompilerParams(
            dimension_semantics=("arbitrary", "arbitrary"), vmem_limit_bytes=40 * MIB),
        name="attn_prompt",
    )(q, kt, v, u)


def _attn_sample_kernel(pt_ref, bias_ref, q_ref, kn_ref, vn_ref, u_ref, *refs, n_pages):
    del pt_ref
    seqs = q_ref.shape[0]
    k_refs = refs[:seqs * n_pages]
    v_refs = refs[seqs * n_pages:2 * seqs * n_pages]
    o_ref = refs[2 * seqs * n_pages]
    steps = q_ref.shape[1]
    page = k_refs[0].shape[-1]
    rows = N_HEADS * steps
    contract_lanes = (((1,), (1,)), ((), ()))

    row_head = lax.broadcasted_iota(jnp.int32, (rows, ATTN_WIDTH), 0) // steps
    lane_head = lax.broadcasted_iota(jnp.int32, (rows, ATTN_WIDTH), 1) // HEAD_DIM
    key_row = lax.broadcasted_iota(jnp.int32, (rows, page), 0)
    bias = jnp.zeros((rows, page), F32)
    for h in range(N_HEADS):
        bias = jnp.where(key_row // steps == h, bias_ref[h], bias)
    new_mask = lax.broadcasted_iota(jnp.int32, (rows, page), 1) < key_row % steps
    masks = [new_mask] + [None] * n_pages
    out_lane_head = lax.broadcasted_iota(jnp.int32, (steps, ATTN_WIDTH), 1) // HEAD_DIM
    pad = jnp.zeros((page - steps, ATTN_WIDTH), F32)

    for j in range(seqs):
        k_pages = k_refs[j * n_pages:(j + 1) * n_pages]
        v_pages = v_refs[j * n_pages:(j + 1) * n_pages]
        q_all = jnp.concatenate([q_ref[j].astype(F32)] * N_HEADS, axis=0)
        q_bd = jnp.where(row_head == lane_head, q_all, 0.0).astype(BF16)

        k_new = jnp.concatenate([kn_ref[j], pad], axis=0).astype(BF16)
        v_new = jnp.concatenate([vn_ref[j], pad], axis=0).astype(BF16)
        zs = [_sb_clamp(lax.dot_general(q_bd, k_new, contract_lanes, preferred_element_type=F32) + bias)]
        for p in reversed(range(n_pages)):
            kt = k_pages[p][...].reshape(ATTN_WIDTH, page).astype(BF16)
            zs.append(_sb_clamp(jnp.dot(q_bd, kt, preferred_element_type=F32) + bias))
        sps = [_sb_softplus(z, mask) for z, mask in zip(zs, masks)]

        later = jnp.dot(jnp.concatenate(sps, axis=0).astype(BF16), u_ref[...], preferred_element_type=F32)
        carry = jnp.zeros((rows, 1), F32)
        ws = []
        for n, (zc, sp, mask) in enumerate(zip(zs, sps, masks)):
            w, carry = _sb_weight(zc, sp, later[n * rows:(n + 1) * rows], carry, mask)
            ws.append(w)

        acc = jnp.dot(ws[0], v_new, preferred_element_type=F32)
        for w, p in zip(ws[1:], reversed(range(n_pages))):
            vt = v_pages[p][...].reshape(ATTN_WIDTH, page).astype(BF16)
            acc = acc + lax.dot_general(w, vt, contract_lanes, preferred_element_type=F32)

        out = jnp.zeros((steps, ATTN_WIDTH), F32)
        for h in range(N_HEADS):
            out = jnp.where(out_lane_head == h, acc[h * steps:(h + 1) * steps], out)
        o_ref[j] = out


def _attn_sample(page_table, bias2, q, k_new, v_new, u, cache_kt, cache_vt):
    n, steps, _ = q.shape
    n_pages = page_table.shape[1]
    page = cache_kt.shape[-1]
    tok_spec = pl.BlockSpec((SAMPLE_SEQS, steps, ATTN_WIDTH), lambda b, pt: (b, 0, 0))

    def page_spec(j, p):
        return pl.BlockSpec((None, None, N_HEADS, HEAD_DIM, page),
                            lambda b, pt: (0, pt[SAMPLE_SEQS * b + j, p], 0, 0, 0))

    page_specs = [page_spec(j, p) for j in range(SAMPLE_SEQS) for p in range(n_pages)]
    grid_spec = pltpu.PrefetchScalarGridSpec(
        num_scalar_prefetch=1,
        grid=(n // SAMPLE_SEQS,),
        in_specs=[pl.BlockSpec(memory_space=pltpu.SMEM), tok_spec, tok_spec, tok_spec,
                  pl.BlockSpec((page, page), lambda b, pt: (0, 0))] + page_specs * 2,
        out_specs=tok_spec,
    )
    n_page_args = SAMPLE_SEQS * n_pages
    return pl.pallas_call(
        functools.partial(_attn_sample_kernel, n_pages=n_pages),
        grid_spec=grid_spec,
        out_shape=jax.ShapeDtypeStruct((n, steps, ATTN_WIDTH), F32),
        compiler_params=pltpu.CompilerParams(
            dimension_semantics=("arbitrary",), vmem_limit_bytes=56 * MIB),
        name="attn_sample",
    )(page_table, bias2, q, k_new, v_new, u, *([cache_kt] * n_page_args), *([cache_vt] * n_page_args))


def _outproj_mlp_kernel(attn_ref, lru_ref, x_ref, wo_ref, gpost_ref, gpre_ref, gout_ref, wu_ref, wd_ref,
                        y_ref):
    mixed = (jnp.dot(attn_ref[0].astype(BF16), wo_ref[:ATTN_WIDTH, :], preferred_element_type=F32)
             + jnp.dot(lru_ref[0], wo_ref[ATTN_WIDTH:, :], preferred_element_type=F32))
    x = x_ref[0] + _rmsnorm(mixed, gpost_ref[...])
    hn = _rmsnorm(x, gpre_ref[...]).astype(BF16)
    acc = jnp.zeros_like(x)
    for c in range(D_FF // FF_CHUNK):
        cols = slice(c * FF_CHUNK, (c + 1) * FF_CHUNK)
        up = jnp.maximum(jnp.dot(hn, wu_ref[:, cols], preferred_element_type=F32), 0.0)
        acc = acc + jnp.dot((up * up).astype(BF16), wd_ref[cols, :], preferred_element_type=F32)
    y_ref[0] = x + _rmsnorm(acc, gout_ref[...])


def _outproj_mlp(attn, lru, x, w_out, g_post, g_pre, g_out, w_up, w_down):
    batch, t, _ = x.shape
    mix_spec = pl.BlockSpec((1, MLP_ROWS, ATTN_WIDTH), lambda b, i: (b, i, 0))
    x_spec = pl.BlockSpec((1, MLP_ROWS, D_MODEL), lambda b, i: (b, i, 0))

    def weight_spec(shape):
        return pl.BlockSpec(shape, lambda b, i: (0, 0), pipeline_mode=pl.Buffered(1))

    return pl.pallas_call(
        _outproj_mlp_kernel,
        grid=(batch, t // MLP_ROWS),
        in_specs=[
            mix_spec, mix_spec, x_spec,
            weight_spec((D_MODEL, D_MODEL)),
            _const_spec((1, D_MODEL)), _const_spec((1, D_MODEL)), _const_spec((1, D_MODEL)),
            weight_spec((D_MODEL, D_FF)), weight_spec((D_FF, D_MODEL)),
        ],
        out_specs=x_spec,
        out_shape=jax.ShapeDtypeStruct((batch, t, D_MODEL), F32),
        compiler_params=pltpu.CompilerParams(
            dimension_semantics=("arbitrary", "arbitrary"), vmem_limit_bytes=56 * MIB),
        name="outproj_mlp",
    )(attn, lru, x, w_out, g_post, g_pre, g_out, w_up, w_down)


def _block_diag(w):
    n, c, d = w.shape
    eye = jnp.eye(n, dtype=w.dtype)
    return (eye[:, None, :, None] * w[:, :, None, :]).reshape(n * c, n * d)


def _later_keys_matrix(n):
    idx = jnp.arange(n)
    return (idx[:, None] > idx[None, :]).astype(BF16)


def _bias_lanes(bias2):
    pieces = []
    rest = bias2
    for _ in range(BIAS_PIECES):
        piece = rest.astype(BF16).astype(F32)
        pieces.append(piece)
        rest = rest - piece
    k_slot = jnp.zeros((N_HEADS, HEAD_SLOT), F32).at[:, HEAD_DIM:HEAD_DIM + BIAS_PIECES].set(
        jnp.stack(pieces, axis=1))
    q_slot = jnp.zeros((N_HEADS, HEAD_SLOT), F32).at[:, HEAD_DIM:HEAD_DIM + BIAS_PIECES].set(1.0)
    return q_slot.reshape(1, -1), k_slot.reshape(1, -1)


def kernel(x_prompt, x_sample, cache_k, cache_v, state_h, state_conv, page_table, meta_tokens,
           g_mix_pre, g_mix_post, g_mlp_pre, g_mlp_post, w_in, sb_bias, conv_w, conv_b, w_gate_a, b_gate_a,
           w_gate_x, b_gate_x, lru_lambda, w_out, w_up, w_down):
    depth = w_in.shape[0]
    assert depth == 1, "single-layer stack"
    batch, seq, _ = x_prompt.shape
    n_dec, dec_seq, _ = x_sample.shape
    assert seq % MLP_ROWS == 0 and (n_dec * dec_seq) % MLP_ROWS == 0 and dec_seq >= CONV_WIDTH - 1
    assert n_dec % SAMPLE_SEQS == 0

    first_row = ROWS - N_META
    n_blk = seq // ROWS + 1
    t_pad = n_blk * ROWS
    meta_blk = jnp.zeros((ROWS, D_MODEL), F32).at[first_row:].set(meta_tokens.astype(F32))

    row = lambda a: a.reshape(1, -1)
    w_in_b = w_in[0].astype(BF16)
    w_out_b = w_out[0].astype(BF16)
    w_up_b = w_up[0].astype(BF16)
    w_down_b = w_down[0].astype(BF16)
    wg = jnp.concatenate([_block_diag(w_gate_a[0]), _block_diag(w_gate_x[0])], axis=1).astype(BF16)
    bg = jnp.concatenate([b_gate_a[0], b_gate_x[0]]).reshape(1, -1)
    lru_args = (conv_w[0], row(conv_b[0]), wg, bg, row(lru_lambda[0]))
    mlp_args = (w_out_b, row(g_mix_post[0]), row(g_mlp_pre[0]), row(g_mlp_post[0]), w_up_b, w_down_b)
    bias2 = sb_bias[0].astype(F32) * LOG2E
    q_extra, k_extra = _bias_lanes(bias2)

    q_p, kt_p, vb_p, k_p, v_p, m_p, h_p, xl_tail_p = _inproj_prompt(
        x_prompt, meta_blk, row(g_mix_pre[0]), w_in_b, q_extra, k_extra, *lru_args, n_blk, first_row)
    attn_p = _attn_prompt(q_p, kt_p, vb_p, _later_keys_matrix(ROWS), first_row)
    y_prompt = _outproj_mlp(attn_p, m_p, x_prompt, *mlp_args)

    xs = x_sample.reshape(n_dec * dec_seq, D_MODEL)
    q_s, k_s, v_s, xl_s, gate_s = _inproj_sample(xs, row(g_mix_pre[0]), w_in_b)
    by_batch = lambda a: a.reshape(n_dec, dec_seq, -1)
    time_major = lambda a: by_batch(a).swapaxes(0, 1)
    m_s, h_s = _lru_sample(time_major(xl_s), time_major(gate_s), state_conv[0].swapaxes(0, 1), state_h[0],
                           *lru_args)
    feature_major = lambda c: jnp.transpose(c, (0, 1, 3, 4, 2))
    attn_s = _attn_sample(page_table, bias2, by_batch(q_s), by_batch(k_s), by_batch(v_s),
                          _later_keys_matrix(cache_k.shape[2]), feature_major(cache_k), feature_major(cache_v))
    y_sample = _outproj_mlp(attn_s.reshape(1, n_dec * dec_seq, ATTN_WIDTH),
                            m_s.swapaxes(0, 1).reshape(1, n_dec * dec_seq, LRU_WIDTH),
                            xs[None], *mlp_args)[0].reshape(x_sample.shape)

    heads = lambda a, lead: a.reshape(*lead, N_HEADS, HEAD_DIM)
    tail = CONV_WIDTH - 1
    return (
        y_prompt,
        y_sample,
        heads(k_p[:, first_row:], (1, batch, seq + N_META)),
        heads(v_p[:, first_row:], (1, batch, seq + N_META)),
        h_p.reshape(1, batch, LRU_WIDTH),
        xl_tail_p[:, SUBLANES - tail:][None],
        heads(k_s, (1, n_dec, dec_seq)),
        heads(v_s, (1, n_dec, dec_seq)),
        h_s[None],
        by_batch(xl_s)[:, dec_seq - tail:][None],
    )
```
